```python
import math
import jax
import jax.numpy as jnp
from jax import lax
import numpy as np


D_MODEL = 1024
BATCH = 16
SEQ = 4096
DEPTH = 2

PLE_DIM = 256
MLP_HIDDEN = 4 * D_MODEL
EPS = 1e-6
LRU_HEADS = 8
LRU_HEAD_DIM = 64
LRU_WIDTH = LRU_HEADS * LRU_HEAD_DIM
LRU_CONV = 4
LRU_C = 8.0
SB_HEADS = 8
SB_HEAD_DIM = 64
SB_WIDTH = SB_HEADS * SB_HEAD_DIM
SB_BLOCK = 128
EVEN_IN = 2 * LRU_WIDTH + 3 * SB_WIDTH
EVEN_OUT = LRU_WIDTH + SB_WIDTH
SSM_HEADS = 16
SSM_HEAD_DIM = 64
SSM_WIDTH = SSM_HEADS * SSM_HEAD_DIM
SSM_GROUPS = 2
SSM_STATE = 128
SSM_CONV = 4
SSM_CHUNK = 128
SSM_XBC = SSM_WIDTH + 2 * SSM_GROUPS * SSM_STATE
CONF_WIDTH = 512
CONF_KERNEL = 31
ODD_IN = SSM_WIDTH + SSM_XBC + SSM_HEADS + 2 * CONF_WIDTH
ODD_OUT = SSM_WIDTH + CONF_WIDTH
N_EVEN = (DEPTH + 1) // 2
N_ODD = DEPTH // 2

kernel_name = 'hybrid_rglru_stickbreak_ssd_conformer'


def rmsnorm(x, g):
    xf = x.astype(jnp.float32)
    y = xf * lax.rsqrt(jnp.mean(xf * xf, axis=-1, keepdims=True) + EPS)
    return (y * g.astype(jnp.float32)).astype(x.dtype)


def group_rmsnorm(x, g, groups):
    xf = x.astype(jnp.float32)
    xg = xf.reshape(xf.shape[:-1] + (groups, xf.shape[-1] // groups))
    xg = xg * lax.rsqrt(jnp.mean(xg * xg, axis=-1, keepdims=True) + EPS)
    return (xg.reshape(xf.shape) * g.astype(jnp.float32)).astype(x.dtype)


def layernorm(x, g, b):
    xf = x.astype(jnp.float32)
    xc = xf - jnp.mean(xf, axis=-1, keepdims=True)
    y = xc * lax.rsqrt(jnp.mean(xc * xc, axis=-1, keepdims=True) + EPS)
    return (y * g.astype(jnp.float32) + b.astype(jnp.float32)).astype(x.dtype)


def causal_dwconv(x, w, b):
    k = w.shape[0]
    y = lax.conv_general_dilated(
        x, w[:, None, :].astype(x.dtype), window_strides=(1,), padding=[(k - 1, 0)],
        dimension_numbers=('NWC', 'WIO', 'NWC'), feature_group_count=x.shape[-1])
    return y + b.astype(x.dtype)


def rg_lru(x, ga_w, ga_b, gx_w, gx_b, lam):
    nb, ns, w = x.shape
    xf = x.astype(jnp.float32)
    xh = xf.reshape(nb, ns, LRU_HEADS, LRU_HEAD_DIM)
    r = jax.nn.sigmoid(jnp.einsum('bshi,hij->bshj', xh, ga_w).reshape(nb, ns, w) + ga_b)
    i = jax.nn.sigmoid(jnp.einsum('bshi,hij->bshj', xh, gx_w).reshape(nb, ns, w) + gx_b)
    log_a = -LRU_C * r * jax.nn.softplus(-lam.astype(jnp.float32))
    a = jnp.exp(log_a)
    u = jnp.sqrt(-jnp.expm1(2.0 * log_a)) * (i * xf)

    def combine(c1, c2):
        return c1[0] * c2[0], c2[0] * c1[1] + c2[1]

    _, h = lax.associative_scan(combine, (a, u), axis=1)
    return h


def stick_breaking(q, k, v):
    nb, ns, nh, dh = q.shape
    scale = 1.0 / math.sqrt(dh)
    qh = q.astype(jnp.float32).transpose(0, 2, 1, 3) * scale
    kh = k.astype(jnp.float32).transpose(0, 2, 1, 3)
    vh = v.astype(jnp.float32).transpose(0, 2, 1, 3)
    outs = []
    for blk in range(ns // SB_BLOCK):
        s0 = blk * SB_BLOCK
        end = s0 + SB_BLOCK
        logits = jnp.einsum('bhqd,bhkd->bhqk', qh[:, :, s0:end], kh[:, :, :end])
        strict = jnp.arange(end)[None, :] < (s0 + jnp.arange(SB_BLOCK))[:, None]
        log_skip = jnp.where(strict, jax.nn.log_sigmoid(-logits), 0.0)
        log_after = lax.cumsum(log_skip, axis=3, reverse=True) - log_skip
        wts = jnp.where(strict, jnp.exp(jax.nn.log_sigmoid(logits) + log_after), 0.0)
        outs.append(jnp.einsum('bhqk,bhkd->bhqd', wts, vh[:, :, :end]))
    o = jnp.concatenate(outs, axis=2)
    return o.transpose(0, 2, 1, 3).reshape(nb, ns, nh * dh)


def segsum_exp(a_cs):
    n = a_cs.shape[-1]
    mask = jnp.tril(jnp.ones((n, n), dtype=bool))
    diff = a_cs[..., :, None] - a_cs[..., None, :]
    return jnp.where(mask, jnp.exp(jnp.where(mask, diff, 0.0)), 0.0)


def ssd(x, dt, a, bm, cm):
    nb, ns, nh, hp = x.shape
    g = bm.shape[2]
    kpg = nh // g
    n = bm.shape[-1]
    cl = SSM_CHUNK
    nc = ns // cl
    xdt = (x * dt[..., None]).reshape(nb, nc, cl, g, kpg, hp)
    da = (dt * a).reshape(nb, nc, cl, g, kpg).transpose(0, 3, 4, 1, 2)
    a_cs = jnp.cumsum(da, axis=-1)
    bc = bm.reshape(nb, nc, cl, g, n)
    cc = cm.reshape(nb, nc, cl, g, n)
    cb = jnp.einsum('bclgn,bcsgn->bcgls', cc, bc)
    y_diag = jnp.einsum('bcgls,bgkcls,bcsgkp->bclgkp', cb, segsum_exp(a_cs), xdt)
    decay_to_end = jnp.exp(a_cs[..., -1:] - a_cs)
    states = jnp.einsum('bclgn,bgkcl,bclgkp->bcgkpn', bc, decay_to_end, xdt)
    chunk_decay = jnp.exp(a_cs[..., -1])

    def step(h, inp):
        st, dec = inp
        return dec[..., None, None] * h + st, h

    h0 = jnp.zeros((nb, g, kpg, hp, n), states.dtype)
    _, prev = lax.scan(step, h0, (jnp.moveaxis(states, 1, 0), jnp.moveaxis(chunk_decay, -1, 0)))
    prev = jnp.moveaxis(prev, 0, 1)
    y_off = jnp.einsum('bclgn,bcgkpn,bgkcl->bclgkp', cc, prev, jnp.exp(a_cs))
    return (y_diag + y_off).reshape(nb, ns, nh, hp)


def even_mixer(h, w_in, conv_w, conv_b, ga_w, ga_b, gx_w, gx_b, lam, w_out):
    nb, ns, _ = h.shape
    proj = h @ w_in
    lru_x, lru_gate, q, k, v = jnp.split(
        proj, [LRU_WIDTH, 2 * LRU_WIDTH, 2 * LRU_WIDTH + SB_WIDTH, 2 * LRU_WIDTH + 2 * SB_WIDTH], axis=-1)
    lru_x = causal_dwconv(lru_x, conv_w, conv_b)
    y_a = rg_lru(lru_x, ga_w, ga_b, gx_w, gx_b, lam) * jax.nn.gelu(lru_gate.astype(jnp.float32))
    shp = (nb, ns, SB_HEADS, SB_HEAD_DIM)
    y_b = stick_breaking(q.reshape(shp), k.reshape(shp), v.reshape(shp))
    y = jnp.concatenate([y_a, y_b], axis=-1).astype(h.dtype)
    return y @ w_out


def odd_mixer(h, w_in, conv_w, conv_b, dt_bias, a_log, d_skip, ssm_norm,
              cm_conv_w, cm_conv_b, cm_ln_g, cm_ln_b, w_out):
    nb, ns, _ = h.shape
    proj = h @ w_in
    z, xbc, dt_raw, glu_in = jnp.split(
        proj, [SSM_WIDTH, SSM_WIDTH + SSM_XBC, SSM_WIDTH + SSM_XBC + SSM_HEADS], axis=-1)
    xbc = jax.nn.silu(causal_dwconv(xbc, conv_w, conv_b).astype(jnp.float32))
    xs, bm, cm = jnp.split(xbc, [SSM_WIDTH, SSM_WIDTH + SSM_GROUPS * SSM_STATE], axis=-1)
    xs = xs.reshape(nb, ns, SSM_HEADS, SSM_HEAD_DIM)
    dt = jax.nn.softplus(dt_raw.astype(jnp.float32) + dt_bias.astype(jnp.float32))
    a = -jnp.exp(a_log.astype(jnp.float32))
    y = ssd(xs, dt, a, bm.reshape(nb, ns, SSM_GROUPS, SSM_STATE), cm.reshape(nb, ns, SSM_GROUPS, SSM_STATE))
    y = y + d_skip.astype(jnp.float32)[:, None] * xs
    y = y.reshape(nb, ns, SSM_WIDTH) * jax.nn.silu(z.astype(jnp.float32))
    y_c = group_rmsnorm(y, ssm_norm, SSM_GROUPS)
    glu_a, glu_b = jnp.split(glu_in, 2, axis=-1)
    u = glu_a * jax.nn.sigmoid(glu_b)
    u = causal_dwconv(u, cm_conv_w, cm_conv_b)
    y_d = jax.nn.silu(layernorm(u, cm_ln_g, cm_ln_b).astype(jnp.float32))
    y = jnp.concatenate([y_c, y_d], axis=-1).astype(h.dtype)
    return y @ w_out


def setup_inputs(seed: int = 0) -> dict:
    key = jax.random.key(seed)
    ks = iter(jax.random.split(key, 40))

    def nrm(shape, scale):
        return jax.random.normal(next(ks), shape, jnp.float32) * scale

    def gain(shape):
        return 1.0 + nrm(shape, 0.05)

    def unif(shape, lo, hi):
        return jax.random.uniform(next(ks), shape, jnp.float32, lo, hi)

    e, o = N_EVEN, N_ODD
    x = nrm((BATCH, SEQ, D_MODEL), 1.0)
    p = nrm((DEPTH, BATCH, SEQ, PLE_DIM), 1.0)
    a0 = unif((e, LRU_WIDTH), 0.9, 0.999)
    dt0 = jnp.exp(unif((o, SSM_HEADS), math.log(1e-3), math.log(1e-1)))
    return {
        'x': x,
        'p': p,
        'ev_w_in': nrm((e, D_MODEL, EVEN_IN), D_MODEL ** -0.5),
        'ev_lru_conv_w': nrm((e, LRU_CONV, LRU_WIDTH), LRU_CONV ** -0.5),
        'ev_lru_conv_b': nrm((e, LRU_WIDTH), 0.02),
        'ev_lru_gate_a_w': nrm((e, LRU_HEADS, LRU_HEAD_DIM, LRU_HEAD_DIM), LRU_HEAD_DIM ** -0.5),
        'ev_lru_gate_a_b': nrm((e, LRU_WIDTH), 0.02),
        'ev_lru_gate_x_w': nrm((e, LRU_HEADS, LRU_HEAD_DIM, LRU_HEAD_DIM), LRU_HEAD_DIM ** -0.5),
        'ev_lru_gate_x_b': nrm((e, LRU_WIDTH), 0.02),
        'ev_lru_lambda': jnp.log(a0) - jnp.log1p(-a0),
        'ev_w_out': nrm((e, EVEN_OUT, D_MODEL), EVEN_OUT ** -0.5),
        'od_w_in': nrm((o, D_MODEL, ODD_IN), D_MODEL ** -0.5),
        'od_ssm_conv_w': nrm((o, SSM_CONV, SSM_XBC), SSM_CONV ** -0.5),
        'od_ssm_conv_b': nrm((o, SSM_XBC), 0.02),
        'od_ssm_dt_bias': dt0 + jnp.log(-jnp.expm1(-dt0)),
        'od_ssm_a_log': jnp.log(unif((o, SSM_HEADS), 1.0, 16.0)),
        'od_ssm_d': gain((o, SSM_HEADS)),
        'od_ssm_norm': gain((o, SSM_WIDTH)),
        'od_cm_conv_w': nrm((o, CONF_KERNEL, CONF_WIDTH), CONF_KERNEL ** -0.5),
        'od_cm_conv_b': nrm((o, CONF_WIDTH), 0.02),
        'od_cm_ln_g': gain((o, CONF_WIDTH)),
        'od_cm_ln_b': nrm((o, CONF_WIDTH), 0.02),
        'od_w_out': nrm((o, ODD_OUT, D_MODEL), ODD_OUT ** -0.5),
        'norm_mix_pre': gain((DEPTH, D_MODEL)),
        'norm_mix_post': gain((DEPTH, D_MODEL)),
        'norm_mlp_pre': gain((DEPTH, D_MODEL)),
        'norm_mlp_post': gain((DEPTH, D_MODEL)),
        'norm_ple': gain((DEPTH, D_MODEL)),
        'mlp_w1': nrm((DEPTH, D_MODEL, MLP_HIDDEN), D_MODEL ** -0.5),
        'mlp_w2': nrm((DEPTH, MLP_HIDDEN, D_MODEL), MLP_HIDDEN ** -0.5),
        'ple_w_proj': nrm((DEPTH, PLE_DIM, D_MODEL), PLE_DIM ** -0.5),
        'ple_w_gate': nrm((DEPTH, D_MODEL, D_MODEL), D_MODEL ** -0.5),
    }


def reference(x, p, ev_w_in, ev_lru_conv_w, ev_lru_conv_b, ev_lru_gate_a_w, ev_lru_gate_a_b,
              ev_lru_gate_x_w, ev_lru_gate_x_b, ev_lru_lambda, ev_w_out,
              od_w_in, od_ssm_conv_w, od_ssm_conv_b, od_ssm_dt_bias, od_ssm_a_log, od_ssm_d,
              od_ssm_norm, od_cm_conv_w, od_cm_conv_b, od_cm_ln_g, od_cm_ln_b, od_w_out,
              norm_mix_pre, norm_mix_post, norm_mlp_pre, norm_mlp_post, norm_ple,
              mlp_w1, mlp_w2, ple_w_proj, ple_w_gate):
    h = x
    for i in range(DEPTH):
        j = i // 2
        hn = rmsnorm(h, norm_mix_pre[i])
        if i % 2 == 0:
            m = even_mixer(hn, ev_w_in[j], ev_lru_conv_w[j], ev_lru_conv_b[j],
                           ev_lru_gate_a_w[j], ev_lru_gate_a_b[j], ev_lru_gate_x_w[j],
                           ev_lru_gate_x_b[j], ev_lru_lambda[j], ev_w_out[j])
        else:
            m = odd_mixer(hn, od_w_in[j], od_ssm_conv_w[j], od_ssm_conv_b[j], od_ssm_dt_bias[j],
                          od_ssm_a_log[j], od_ssm_d[j], od_ssm_norm[j], od_cm_conv_w[j],
                          od_cm_conv_b[j], od_cm_ln_g[j], od_cm_ln_b[j], od_w_out[j])
        h = h + rmsnorm(m, norm_mix_post[i])
        hn = rmsnorm(h, norm_mlp_pre[i])
        f = jnp.square(jax.nn.relu(hn @ mlp_w1[i])) @ mlp_w2[i]
        h = h + rmsnorm(f, norm_mlp_post[i])
        gate = jax.nn.sigmoid(h @ ple_w_gate[i])
        emb = p[i] @ ple_w_proj[i]
        h = h + rmsnorm(gate * emb, norm_ple[i])
    return h
```

```python
import functools
import math

import jax
import jax.numpy as jnp
from jax import lax
from jax.experimental import pallas as pl
from jax.experimental.pallas import tpu as pltpu

F32 = jnp.float32
BF16 = jnp.bfloat16

EPS = 1e-6
LRU_HEADS = 8
LRU_C = 8.0
SB_HEAD_DIM = 64
SSM_HEADS = 16
SSM_HEAD_DIM = 64
SSM_GROUPS = 2
SSM_STATE = 128
SSM_CHUNK = 128

LANES = 128
HALO = 8
CONF_HALO = 32
VMEM_LIMIT = 56 * 1024 * 1024


def _params(*sem):
    return pltpu.CompilerParams(dimension_semantics=sem, vmem_limit_bytes=VMEM_LIMIT)


def _const_spec(shape):
    zeros = (0,) * len(shape)
    return pl.BlockSpec(shape, lambda *_: zeros, pipeline_mode=pl.Buffered(1))


def _rms(x, g):
    return x * lax.rsqrt(jnp.mean(x * x, axis=-1, keepdims=True) + EPS) * g


def _dot(a, b):
    return jnp.dot(a, b, preferred_element_type=F32)


def _split3(x):
    h1 = x.astype(BF16)
    r1 = x - h1.astype(F32)
    h2 = r1.astype(BF16)
    h3 = (r1 - h2.astype(F32)).astype(BF16)
    return h1, h2, h3


def _softplus(x):
    return jnp.maximum(x, 0.0) + jnp.log1p(jnp.exp(-jnp.abs(x)))


def _sigmoid(x):
    return 1.0 / (1.0 + jnp.exp(-x))


def _silu(x):
    return x * _sigmoid(x)


def _norm_proj_kernel(h_ref, g_ref, *refs):
    n = len(refs) // 2
    hn = _rms(h_ref[...], g_ref[...]).astype(BF16)
    for w_ref, o_ref in zip(refs[:n], refs[n:]):
        o_ref[...] = _dot(hn, w_ref[...]).astype(o_ref.dtype)


def _norm_proj(h, g, ws, dtypes, tm):
    t, d = h.shape
    return pl.pallas_call(
        _norm_proj_kernel,
        grid=(t // tm,),
        in_specs=[pl.BlockSpec((tm, d), lambda i: (i, 0)), _const_spec((1, d))]
        + [_const_spec(w.shape) for w in ws],
        out_specs=[pl.BlockSpec((tm, w.shape[1]), lambda i: (i, 0)) for w in ws],
        out_shape=[jax.ShapeDtypeStruct((t, w.shape[1]), dt) for w, dt in zip(ws, dtypes)],
        compiler_params=_params("parallel"),
        name="norm_proj",
    )(h, g.reshape(1, d), *ws)


def _lru_kernel(x_ref, gate_ref, cw_ref, cb_ref, wa_ref, ba_ref, wx_ref, bx_ref, lam_ref,
                o_ref, xext_ref, h_ref):
    ts, w = x_ref.shape[1], x_ref.shape[2]

    @pl.when(pl.program_id(1) == 0)
    def _():
        xext_ref[0:HALO, :] = jnp.zeros((HALO, w), F32)
        h_ref[...] = jnp.zeros_like(h_ref)

    xext_ref[HALO:, :] = x_ref[0]
    taps = cw_ref.shape[0]
    xc = cb_ref[...] + sum(
        cw_ref[k:k + 1, :] * xext_ref[pl.ds(HALO - taps + 1 + k, ts), :] for k in range(taps))
    xext_ref[0:HALO, :] = xext_ref[ts:ts + HALO, :]

    xb = xc.astype(BF16)
    r = _sigmoid(_dot(xb, wa_ref[...]) + ba_ref[...])
    i = _sigmoid(_dot(xb, wx_ref[...]) + bx_ref[...])
    log_a = (-LRU_C) * r * _softplus(-lam_ref[...])
    a = jnp.exp(log_a)
    u = jnp.sqrt(-jnp.tanh(log_a) * (a * a + 1.0)) * (i * xc)

    row = lax.broadcasted_iota(jnp.int32, (ts, w), 0)
    d = 1
    while d < ts:
        keep = row >= d
        a_prev = jnp.where(keep, pltpu.roll(a, d, 0), 1.0)
        u_prev = jnp.where(keep, pltpu.roll(u, d, 0), 0.0)
        u = a * u_prev + u
        a = a * a_prev
        d *= 2
    hs = u + a * h_ref[...]
    h_ref[...] = hs[ts - 1:ts, :]
    o_ref[0] = (hs * jax.nn.gelu(gate_ref[0])).astype(o_ref.dtype)


def _block_diag(w):
    nh, ni, nj = w.shape
    eye = jnp.eye(nh, dtype=w.dtype)
    return (w[:, :, None, :] * eye[:, None, :, None]).reshape(nh * ni, nh * nj)


def _lru(x, gate, conv_w, conv_b, ga_w, ga_b, gx_w, gx_b, lam, ts):
    b, s, w = x.shape
    row = lambda v: v.reshape(1, w).astype(F32)
    seq_spec = pl.BlockSpec((1, ts, w), lambda bi, si: (bi, si, 0))
    return pl.pallas_call(
        _lru_kernel,
        grid=(b, s // ts),
        in_specs=[seq_spec, seq_spec, _const_spec(conv_w.shape), _const_spec((1, w)),
                  _const_spec((w, w)), _const_spec((1, w)), _const_spec((w, w)), _const_spec((1, w)),
                  _const_spec((1, w))],
        out_specs=seq_spec,
        out_shape=jax.ShapeDtypeStruct((b, s, w), BF16),
        scratch_shapes=[pltpu.VMEM((ts + HALO, w), F32), pltpu.VMEM((1, w), F32)],
        compiler_params=_params("parallel", "arbitrary"),
        name="rg_lru",
    )(x, gate, conv_w, row(conv_b), _block_diag(ga_w).astype(BF16), row(ga_b),
      _block_diag(gx_w).astype(BF16), row(gx_b), row(lam))


def _sb_kernel(q_ref, k_ref, v_ref, o_ref, *, scale):
    tq = q_ref.shape[1]
    tk = tq
    qb = pl.program_id(2)
    lane = lax.broadcasted_iota(jnp.int32, (tq, LANES), 1)
    t_idx = lax.broadcasted_iota(jnp.int32, (tq, tk), 0)
    s_idx = lax.broadcasted_iota(jnp.int32, (tq, tk), 1)
    strict = s_idx < t_idx
    suffix = (t_idx > s_idx).astype(BF16)
    q = q_ref[0]

    def block(qh, kb, vh, acc_log, masked):
        k_blk = k_ref[0, pl.ds(kb * tk, tk), :]
        logits = lax.dot_general(qh, k_blk, (((1,), (1,)), ((), ())), preferred_element_type=F32) * scale
        log_skip = -_softplus(logits)
        if masked:
            log_skip = jnp.where(strict, log_skip, 0.0)
        hi = log_skip.astype(BF16)
        lo = (log_skip - hi.astype(F32)).astype(BF16)
        after = _dot(hi, suffix) + _dot(lo, suffix)
        wts = jnp.exp(logits + log_skip + after + acc_log)
        if masked:
            wts = jnp.where(strict, wts, 0.0)
        pv = _dot(wts.astype(BF16), vh(kb))
        acc_log = acc_log + (after[:, 0:1] + log_skip[:, 0:1])
        return pv, acc_log

    out = jnp.zeros((tq, LANES), F32)
    for head in range(LANES // SB_HEAD_DIM):
        mine = (lane // SB_HEAD_DIM) == head
        qh = jnp.where(mine, q, jnp.zeros_like(q))

        def vh(kb, mine=mine):
            v_blk = v_ref[0, pl.ds(kb * tk, tk), :]
            return jnp.where(mine, v_blk, jnp.zeros_like(v_blk))

        pv, acc_log = block(qh, qb, vh, jnp.zeros((tq, 1), F32), True)

        def body(i, carry, qh=qh, vh=vh):
            acc, acc_log = carry
            pv, acc_log = block(qh, qb - 1 - i, vh, acc_log, False)
            return acc + pv, acc_log

        acc, _ = lax.fori_loop(0, qb, body, (pv, acc_log))
        out = out + acc
    o_ref[0] = out.astype(o_ref.dtype)


def _stick_breaking(q, k, v, tq):
    b, s, w = q.shape
    kv_spec = pl.BlockSpec((1, s, LANES), lambda bi, hi, qi: (bi, 0, hi))
    q_spec = pl.BlockSpec((1, tq, LANES), lambda bi, hi, qi: (bi, qi, hi))
    return pl.pallas_call(
        functools.partial(_sb_kernel, scale=1.0 / math.sqrt(SB_HEAD_DIM)),
        grid=(b, w // LANES, s // tq),
        in_specs=[q_spec, kv_spec, kv_spec],
        out_specs=q_spec,
        out_shape=jax.ShapeDtypeStruct((b, s, w), BF16),
        compiler_params=_params("parallel", "parallel", "arbitrary"),
        name="stick_breaking",
    )(q, k, v)


def _ssd_kernel(xbc_ref, z_ref, dt_ref, cw_ref, cb_ref, dtb_ref, a_ref, dskip_ref, gn_ref,
                o_ref, xext_ref, state_ref):
    cl = xbc_ref.shape[1]
    width = z_ref.shape[2]
    gw = width // SSM_GROUPS
    n = SSM_STATE

    @pl.when(pl.program_id(1) == 0)
    def _():
        xext_ref[0:HALO, :] = jnp.zeros((HALO, xext_ref.shape[1]), F32)
        state_ref[...] = jnp.zeros_like(state_ref)

    xext_ref[HALO:, :] = xbc_ref[0]
    taps = cw_ref.shape[0]
    xc = cb_ref[...] + sum(
        cw_ref[k:k + 1, :] * xext_ref[pl.ds(HALO - taps + 1 + k, cl), :] for k in range(taps))
    xext_ref[0:HALO, :] = xext_ref[cl:cl + HALO, :]
    xc = _silu(xc)
    xs = xc[:, :width]

    dt = _softplus(dt_ref[0] + dtb_ref[...])
    da = dt * a_ref[...]
    l_idx = lax.broadcasted_iota(jnp.int32, (cl, cl), 0)
    s_idx = lax.broadcasted_iota(jnp.int32, (cl, cl), 1)
    causal = l_idx >= s_idx
    tril = causal.astype(BF16)
    a_cs = sum(_dot(tril, piece) for piece in _split3(da))
    a_cs_t = a_cs.T

    lane = lax.broadcasted_iota(jnp.int32, (cl, LANES), 1)
    low = lane < SSM_HEAD_DIM

    def expand(cols):
        return jnp.concatenate(
            [jnp.where(low, jnp.broadcast_to(cols[:, 2 * i:2 * i + 1], (cl, LANES)),
                       jnp.broadcast_to(cols[:, 2 * i + 1:2 * i + 2], (cl, LANES)))
             for i in range(SSM_HEADS // 2)], axis=1)

    dt_x = expand(dt)
    acs_x = expand(a_cs)
    decay_in = jnp.exp(acs_x)
    decay_out = jnp.exp(acs_x[cl - 1:cl, :] - acs_x)
    xdt = xs * dt_x

    heads_per_group = SSM_HEADS // SSM_GROUPS
    ys = []
    for g in range(SSM_GROUPS):
        bm = xc[:, width + g * n:width + (g + 1) * n]
        cm = xc[:, width + SSM_GROUPS * n + g * n:width + SSM_GROUPS * n + (g + 1) * n].astype(BF16)
        bm_t = bm.T.astype(BF16)
        cb = _dot(cm, bm_t)
        cols = slice(g * gw, (g + 1) * gw)
        y_diag = []
        for pair in range(heads_per_group // 2):
            ms = []
            for h in range(g * heads_per_group + 2 * pair, g * heads_per_group + 2 * pair + 2):
                seg = a_cs[:, h:h + 1] - a_cs_t[h:h + 1, :]
                decay = jnp.where(causal, jnp.exp(jnp.where(causal, seg, 0.0)), 0.0)
                ms.append((cb * decay).astype(BF16))
            c0 = g * gw + pair * LANES
            x_pair = xdt[:, c0:c0 + LANES]
            x_bd = jnp.concatenate([jnp.where(low, x_pair, 0.0), jnp.where(low, 0.0, x_pair)], axis=0)
            y_diag.append(_dot(jnp.concatenate(ms, axis=1), x_bd.astype(BF16)))
        y_diag = jnp.concatenate(y_diag, axis=1)
        state = state_ref[g]
        y_off = _dot(cm, state.astype(BF16)) * decay_in[:, cols]
        state_ref[g] = decay_in[cl - 1:cl, cols] * state + _dot(
            bm_t, (xdt[:, cols] * decay_out[:, cols]).astype(BF16))
        ys.append(y_diag + y_off)
    y = jnp.concatenate(ys, axis=1) + dskip_ref[...] * xs
    y = y * _silu(z_ref[0])
    y = jnp.concatenate(
        [y[:, g * gw:(g + 1) * gw]
         * lax.rsqrt(jnp.mean(jnp.square(y[:, g * gw:(g + 1) * gw]), axis=-1, keepdims=True) + EPS)
         for g in range(SSM_GROUPS)], axis=1)
    o_ref[0] = (y * gn_ref[...]).astype(o_ref.dtype)


def _ssd(xbc, z, dt_raw, conv_w, conv_b, dt_bias, a_log, d_skip, ssm_norm):
    b, s, xw = xbc.shape
    width = z.shape[2]
    cl = SSM_CHUNK
    pad = lambda v: jnp.pad(v.astype(F32), (0, LANES - v.shape[0])).reshape(1, LANES)
    seq = lambda w: pl.BlockSpec((1, cl, w), lambda bi, ci: (bi, ci, 0))
    return pl.pallas_call(
        _ssd_kernel,
        grid=(b, s // cl),
        in_specs=[seq(xw), seq(width), seq(LANES), _const_spec(conv_w.shape), _const_spec((1, xw)),
                  _const_spec((1, LANES)), _const_spec((1, LANES)), _const_spec((1, width)),
                  _const_spec((1, width))],
        out_specs=seq(width),
        out_shape=jax.ShapeDtypeStruct((b, s, width), BF16),
        scratch_shapes=[pltpu.VMEM((cl + HALO, xw), F32),
                        pltpu.VMEM((SSM_GROUPS, SSM_STATE, width // SSM_GROUPS), F32)],
        compiler_params=_params("parallel", "arbitrary"),
        name="ssd",
    )(xbc, z, dt_raw, conv_w, conv_b.reshape(1, xw), pad(dt_bias), pad(-jnp.exp(a_log.astype(F32))),
      jnp.repeat(d_skip.astype(F32), SSM_HEAD_DIM).reshape(1, width), ssm_norm.reshape(1, width))


def _conformer_kernel(glu_ref, cw_ref, cb_ref, g_ref, b_ref, o_ref, uext_ref):
    ts, w = o_ref.shape[1], o_ref.shape[2]

    @pl.when(pl.program_id(1) == 0)
    def _():
        uext_ref[0:CONF_HALO, :] = jnp.zeros((CONF_HALO, w), F32)

    uext_ref[CONF_HALO:, :] = glu_ref[0, :, :w] * _sigmoid(glu_ref[0, :, w:])
    taps = cw_ref.shape[0]
    y = cb_ref[...] + sum(
        cw_ref[k:k + 1, :] * uext_ref[pl.ds(CONF_HALO - taps + 1 + k, ts), :] for k in range(taps))
    uext_ref[0:CONF_HALO, :] = uext_ref[ts:ts + CONF_HALO, :]
    yc = y - jnp.mean(y, axis=-1, keepdims=True)
    yn = yc * lax.rsqrt(jnp.mean(yc * yc, axis=-1, keepdims=True) + EPS)
    o_ref[0] = _silu(yn * g_ref[...] + b_ref[...]).astype(o_ref.dtype)


def _conformer(glu, conv_w, conv_b, ln_g, ln_b, ts):
    b, s, w2 = glu.shape
    w = w2 // 2
    return pl.pallas_call(
        _conformer_kernel,
        grid=(b, s // ts),
        in_specs=[pl.BlockSpec((1, ts, w2), lambda bi, si: (bi, si, 0)), _const_spec(conv_w.shape),
                  _const_spec((1, w)), _const_spec((1, w)), _const_spec((1, w))],
        out_specs=pl.BlockSpec((1, ts, w), lambda bi, si: (bi, si, 0)),
        out_shape=jax.ShapeDtypeStruct((b, s, w), BF16),
        scratch_shapes=[pltpu.VMEM((ts + CONF_HALO, w), F32)],
        compiler_params=_params("parallel", "arbitrary"),
        name="conformer",
    )(glu, conv_w, conv_b.reshape(1, w), ln_g.reshape(1, w), ln_b.reshape(1, w))


def _post_kernel(*refs, n_mix, hidden_chunk):
    h_ref, p_ref = refs[0], refs[1]
    y_refs = refs[2:2 + n_mix]
    wo_refs = refs[2 + n_mix:2 + 2 * n_mix]
    (g_post_ref, g_pre_ref, w1_ref, w2_ref, g_mlp_ref, wg_ref, wp_ref, g_ple_ref, o_ref) = refs[2 + 2 * n_mix:]

    m = sum(_dot(y_ref[...], wo_ref[...]) for y_ref, wo_ref in zip(y_refs, wo_refs))
    h = h_ref[...] + _rms(m, g_post_ref[...])

    hn = _rms(h, g_pre_ref[...]).astype(BF16)
    f = jnp.zeros_like(h)
    for c in range(0, w1_ref.shape[1], hidden_chunk):
        a = jnp.maximum(_dot(hn, w1_ref[:, c:c + hidden_chunk]), 0.0)
        f = f + _dot((a * a).astype(BF16), w2_ref[c:c + hidden_chunk, :])
    h = h + _rms(f, g_mlp_ref[...])

    gate = _sigmoid(_dot(h.astype(BF16), wg_ref[...]))
    emb = _dot(p_ref[...].astype(BF16), wp_ref[...])
    o_ref[...] = h + _rms(gate * emb, g_ple_ref[...])


def _post(h, p, ys, w_out, g_post, g_pre, w1, w2, g_mlp, wg, wp, g_ple, tm):
    t, d = h.shape
    wos, r0 = [], 0
    for y in ys:
        wos.append(w_out[r0:r0 + y.shape[1]].astype(BF16))
        r0 += y.shape[1]
    rows = lambda w: pl.BlockSpec((tm, w), lambda i: (i, 0))
    gains = [g.reshape(1, d).astype(F32) for g in (g_post, g_pre, g_mlp, g_ple)]
    return pl.pallas_call(
        functools.partial(_post_kernel, n_mix=len(ys), hidden_chunk=min(1024, w1.shape[1])),
        grid=(t // tm,),
        in_specs=[rows(d), rows(p.shape[1])] + [rows(y.shape[1]) for y in ys]
        + [_const_spec(w.shape) for w in wos]
        + [_const_spec((1, d)), _const_spec((1, d)), _const_spec(w1.shape), _const_spec(w2.shape),
           _const_spec((1, d)), _const_spec(wg.shape), _const_spec(wp.shape), _const_spec((1, d))],
        out_specs=rows(d),
        out_shape=jax.ShapeDtypeStruct((t, d), F32),
        compiler_params=_params("parallel"),
        name="post_mixer",
    )(h, p, *ys, *wos, gains[0], gains[1], w1.astype(BF16), w2.astype(BF16), gains[2],
      wg.astype(BF16), wp.astype(BF16), gains[3])


def _even_layer(h, b, s, w_in, conv_w, conv_b, ga_w, ga_b, gx_w, gx_b, lam, g_pre, tm, ts, tq):
    lw = lam.shape[0]
    sw = (w_in.shape[1] - 2 * lw) // 3
    edges = [0, lw, 2 * lw, 2 * lw + sw, 2 * lw + 2 * sw, 2 * lw + 3 * sw]
    ws = [w_in[:, a:c].astype(BF16) for a, c in zip(edges[:-1], edges[1:])]
    lru_x, gate, q, k, v = _norm_proj(h, g_pre, ws, [F32, F32, BF16, BF16, BF16], tm)
    seq = lambda a: a.reshape(b, s, a.shape[1])
    y_a = _lru(seq(lru_x), seq(gate), conv_w, conv_b, ga_w, ga_b, gx_w, gx_b, lam, ts)
    y_b = _stick_breaking(seq(q), seq(k), seq(v), tq)
    return [y_a.reshape(b * s, lw), y_b.reshape(b * s, sw)]


def _odd_layer(h, b, s, w_in, conv_w, conv_b, dt_bias, a_log, d_skip, ssm_norm,
               cm_conv_w, cm_conv_b, cm_ln_g, cm_ln_b, g_pre, tm, ts):
    width = ssm_norm.shape[0]
    xw = conv_w.shape[1]
    nh = dt_bias.shape[0]
    cw = cm_ln_g.shape[0]
    e1, e2, e3 = width, width + xw, width + xw + nh
    w_dt = jnp.pad(w_in[:, e2:e3], ((0, 0), (0, LANES - nh)))
    ws = [w.astype(BF16) for w in (w_in[:, :e1], w_in[:, e1:e2], w_dt, w_in[:, e3:])]
    z, xbc, dt_raw, glu = _norm_proj(h, g_pre, ws, [F32, F32, F32, F32], tm)
    seq = lambda a: a.reshape(b, s, a.shape[1])
    y_c = _ssd(seq(xbc), seq(z), seq(dt_raw), conv_w, conv_b, dt_bias, a_log, d_skip, ssm_norm)
    y_d = _conformer(seq(glu), cm_conv_w, cm_conv_b, cm_ln_g, cm_ln_b, ts)
    return [y_c.reshape(b * s, width), y_d.reshape(b * s, cw)]


def kernel(x, p, ev_w_in, ev_lru_conv_w, ev_lru_conv_b, ev_lru_gate_a_w, ev_lru_gate_a_b, ev_lru_gate_x_w, ev_lru_gate_x_b, ev_lru_lambda, ev_w_out, od_w_in, od_ssm_conv_w, od_ssm_conv_b, od_ssm_dt_bias, od_ssm_a_log, od_ssm_d, od_ssm_norm, od_cm_conv_w, od_cm_conv_b, od_cm_ln_g, od_cm_ln_b, od_w_out, norm_mix_pre, norm_mix_post, norm_mlp_pre, norm_mlp_post, norm_ple, mlp_w1, mlp_w2, ple_w_proj, ple_w_gate):
    b, s, d = x.shape
    depth = p.shape[0]
    t = b * s
    tm = min(512, t)
    ts = min(256, s)
    tq = min(128, s)
    h = x.reshape(t, d)
    for i in range(depth):
        j = i // 2
        if i % 2 == 0:
            ys = _even_layer(h, b, s, ev_w_in[j], ev_lru_conv_w[j], ev_lru_conv_b[j], ev_lru_gate_a_w[j],
                             ev_lru_gate_a_b[j], ev_lru_gate_x_w[j], ev_lru_gate_x_b[j], ev_lru_lambda[j],
                             norm_mix_pre[i], tm, ts, tq)
            w_out = ev_w_out[j]
        else:
            ys = _odd_layer(h, b, s, od_w_in[j], od_ssm_conv_w[j], od_ssm_conv_b[j], od_ssm_dt_bias[j],
                            od_ssm_a_log[j], od_ssm_d[j], od_ssm_norm[j], od_cm_conv_w[j], od_cm_conv_b[j],
                            od_cm_ln_g[j], od_cm_ln_b[j], norm_mix_pre[i], tm, ts)
            w_out = od_w_out[j]
        h = _post(h, p[i].reshape(t, p.shape[3]), ys, w_out, norm_mix_post[i], norm_mlp_pre[i],
                  mlp_w1[i], mlp_w2[i], norm_mlp_post[i], ple_w_gate[i], ple_w_proj[i], norm_ple[i], tm)
    return h.reshape(b, s, d)
```

```python
import functools
import math

import jax
import jax.numpy as jnp
from jax import lax
from jax.experimental import pallas as pl
from jax.experimental.pallas import tpu as pltpu

F32 = jnp.float32
BF16 = jnp.bfloat16

EPS = 1e-6
LOG2E = 1.4426950408889634
LRU_HEADS = 8
LRU_C = 8.0
SB_HEAD_DIM = 64
SSM_HEADS = 16
SSM_HEAD_DIM = 64
SSM_GROUPS = 2
SSM_STATE = 128
SSM_CHUNK = 128
SB_KEYS = 256

LANES = 128
SUBLANES = 8
HALO = 8
CONF_HALO = 32
VMEM_LIMIT = 56 * 1024 * 1024


def _params(*sem):
    return pltpu.CompilerParams(dimension_semantics=sem, vmem_limit_bytes=VMEM_LIMIT)


def _const_spec(shape):
    zeros = (0,) * len(shape)
    return pl.BlockSpec(shape, lambda *_: zeros, pipeline_mode=pl.Buffered(1))


def _rms(x, g):
    return x * lax.rsqrt(jnp.mean(x * x, axis=-1, keepdims=True) + EPS) * g


def _dot(a, b):
    return jnp.dot(a, b, preferred_element_type=F32)


def _split3(x):
    h1 = x.astype(BF16)
    r1 = x - h1.astype(F32)
    h2 = r1.astype(BF16)
    h3 = (r1 - h2.astype(F32)).astype(BF16)
    return h1, h2, h3


def _softplus(x):
    return jnp.maximum(x, 0.0) + jnp.log1p(jnp.exp(-jnp.abs(x)))


def _sigmoid(x):
    return 1.0 / (1.0 + jnp.exp(-x))


def _silu(x):
    return x * _sigmoid(x)


def _norm_proj_kernel(h_ref, g_ref, *refs):
    n = len(refs) // 2
    hn = _rms(h_ref[...], g_ref[...]).astype(BF16)
    for w_ref, o_ref in zip(refs[:n], refs[n:]):
        o_ref[...] = _dot(hn, w_ref[...]).astype(o_ref.dtype)


def _norm_proj(h, g, ws, dtypes, tm):
    t, d = h.shape
    return pl.pallas_call(
        _norm_proj_kernel,
        grid=(t // tm,),
        in_specs=[pl.BlockSpec((tm, d), lambda i: (i, 0)), _const_spec((1, d))]
        + [_const_spec(w.shape) for w in ws],
        out_specs=[pl.BlockSpec((tm, w.shape[1]), lambda i: (i, 0)) for w in ws],
        out_shape=[jax.ShapeDtypeStruct((t, w.shape[1]), dt) for w, dt in zip(ws, dtypes)],
        compiler_params=_params("parallel"),
        name="norm_proj",
    )(h, g.reshape(1, d), *ws)


def _lru_kernel(x_ref, gate_ref, cw_ref, cb_ref, wa_ref, ba_ref, wx_ref, bx_ref, lam_ref,
                o_ref, xext_ref, h_ref):
    ts, w = x_ref.shape[1], x_ref.shape[2]

    @pl.when(pl.program_id(1) == 0)
    def _():
        xext_ref[0:HALO, :] = jnp.zeros((HALO, w), F32)
        h_ref[...] = jnp.zeros_like(h_ref)

    xext_ref[HALO:, :] = x_ref[0]
    taps = cw_ref.shape[0]
    xc = cb_ref[...] + sum(
        cw_ref[k:k + 1, :] * xext_ref[pl.ds(HALO - taps + 1 + k, ts), :] for k in range(taps))
    xext_ref[0:HALO, :] = xext_ref[ts:ts + HALO, :]

    xb = xc.astype(BF16)
    r = _sigmoid(_dot(xb, wa_ref[...]) + ba_ref[...])
    i = _sigmoid(_dot(xb, wx_ref[...]) + bx_ref[...])
    log_a = (-LRU_C) * r * _softplus(-lam_ref[...])
    a = jnp.exp(log_a)
    u = jnp.sqrt(-jnp.tanh(log_a) * (a * a + 1.0)) * (i * xc)

    row = lax.broadcasted_iota(jnp.int32, (ts, w), 0)
    d = 1
    while d < ts:
        keep = row >= d
        a_prev = jnp.where(keep, pltpu.roll(a, d, 0), 1.0)
        u_prev = jnp.where(keep, pltpu.roll(u, d, 0), 0.0)
        u = a * u_prev + u
        a = a * a_prev
        d *= 2
    hs = u + a * h_ref[...]
    h_ref[...] = hs[ts - 1:ts, :]
    o_ref[0] = (hs * jax.nn.gelu(gate_ref[0])).astype(o_ref.dtype)


def _block_diag(w):
    nh, ni, nj = w.shape
    eye = jnp.eye(nh, dtype=w.dtype)
    return (w[:, :, None, :] * eye[:, None, :, None]).reshape(nh * ni, nh * nj)


def _lru(x, gate, conv_w, conv_b, ga_w, ga_b, gx_w, gx_b, lam, ts):
    b, s, w = x.shape
    row = lambda v: v.reshape(1, w).astype(F32)
    seq_spec = pl.BlockSpec((1, ts, w), lambda bi, si: (bi, si, 0))
    return pl.pallas_call(
        _lru_kernel,
        grid=(b, s // ts),
        in_specs=[seq_spec, seq_spec, _const_spec(conv_w.shape), _const_spec((1, w)),
                  _const_spec((w, w)), _const_spec((1, w)), _const_spec((w, w)), _const_spec((1, w)),
                  _const_spec((1, w))],
        out_specs=seq_spec,
        out_shape=jax.ShapeDtypeStruct((b, s, w), BF16),
        scratch_shapes=[pltpu.VMEM((ts + HALO, w), F32), pltpu.VMEM((1, w), F32)],
        compiler_params=_params("parallel", "arbitrary"),
        name="rg_lru",
    )(x, gate, conv_w, row(conv_b), _block_diag(ga_w).astype(BF16), row(ga_b),
      _block_diag(gx_w).astype(BF16), row(gx_b), row(lam))


def _sb_kernel(q_ref, k_ref, v_ref, cum_ref, o_ref, acc_ref, skip_ref, *z_refs, scale):
    tq = q_ref.shape[1]
    tk = SB_KEYS
    n_heads = LANES // SB_HEAD_DIM
    n_blocks = tq // tk
    qb = pl.program_id(2)
    lane = lax.broadcasted_iota(jnp.int32, (1, LANES), 1)
    ahead = (lax.broadcasted_iota(jnp.int32, (tq, tk), 1) - lax.broadcasted_iota(jnp.int32, (tq, tk), 0))
    q = q_ref[0] * scale
    qs = [jnp.where((lane // SB_HEAD_DIM) == h, q, jnp.zeros_like(q)) for h in range(n_heads)]
    cum = cum_ref[...]

    def scores(kb, z_ref):
        k_blk = k_ref[0, pl.ds(kb * tk, tk), :]
        for h in range(n_heads):
            z_ref[h] = lax.dot_general(
                qs[h], k_blk, (((1,), (1,)), ((), ())), preferred_element_type=F32) * LOG2E

    def block(kb, z_ref, masked):
        v_blk = v_ref[0, pl.ds(kb * tk, tk), :]
        if masked:
            strict = ahead < qb * tq - kb * tk
        wts, vs = [], []
        for h in range(n_heads):
            z = z_ref[h]
            minus_abs = lax.bitcast_convert_type(
                lax.bitcast_convert_type(z, jnp.uint32) | jnp.uint32(0x80000000), F32)
            sp = jnp.maximum(z, 0.0) + jnp.log2(1.0 + jnp.exp2(minus_abs))
            log_beta = z - sp
            if masked:
                sp = jnp.where(strict, sp, 0.0)
            sp = sp.astype(BF16)
            after = _dot(sp, cum)
            skipped = skip_ref[h]
            w = jnp.exp2(log_beta - after - skipped)
            skip_ref[h] = skipped + (after[:, 0:1] + sp[:, 0:1].astype(F32))
            if masked:
                w = jnp.where(strict, w, 0.0)
            wts.append(w.astype(BF16))
            vs.append(jnp.where((lane // SB_HEAD_DIM) == h, v_blk, jnp.zeros_like(v_blk)))
        acc_ref[...] += _dot(jnp.concatenate(wts, axis=1), jnp.concatenate(vs, axis=0))

    acc_ref[...] = jnp.zeros_like(acc_ref)
    skip_ref[...] = jnp.zeros_like(skip_ref)
    for d in reversed(range(n_blocks)):
        scores(qb * n_blocks + d, z_refs[d])
        block(qb * n_blocks + d, z_refs[d], True)

    def key_block(i, d):
        return jnp.maximum((qb - i) * n_blocks - 1 - d, 0)

    def trip(i, _):
        for d in range(n_blocks):
            nxt = (d + 1) % n_blocks
            scores(key_block(i + (d + 1) // n_blocks, nxt), z_refs[nxt])
            block(key_block(i, d), z_refs[d], False)
        return 0

    scores(key_block(0, 0), z_refs[0])
    lax.fori_loop(0, qb, trip, 0)
    o_ref[0] = acc_ref[...].astype(o_ref.dtype)


def _stick_breaking(q, k, v, tq):
    b, s, w = q.shape
    n_heads = LANES // SB_HEAD_DIM
    cum = (jnp.arange(SB_KEYS)[:, None] > jnp.arange(SB_KEYS)[None, :]).astype(BF16)
    kv_spec = pl.BlockSpec((1, s, LANES), lambda bi, hi, qi: (bi, 0, hi))
    q_spec = pl.BlockSpec((1, tq, LANES), lambda bi, hi, qi: (bi, qi, hi))
    return pl.pallas_call(
        functools.partial(_sb_kernel, scale=1.0 / math.sqrt(SB_HEAD_DIM)),
        grid=(b, w // LANES, s // tq),
        in_specs=[q_spec, kv_spec, kv_spec, _const_spec(cum.shape)],
        out_specs=q_spec,
        out_shape=jax.ShapeDtypeStruct((b, s, w), BF16),
        scratch_shapes=[pltpu.VMEM((tq, LANES), F32), pltpu.VMEM((n_heads, tq, 1), F32)]
        + [pltpu.VMEM((n_heads, tq, SB_KEYS), F32) for _ in range(tq // SB_KEYS)],
        compiler_params=_params("parallel", "parallel", "arbitrary"),
        name="stick_breaking",
    )(q, k, v, cum)


def _ssd_kernel(xbc_ref, z_ref, dt_ref, cw_ref, cb_ref, dtb_ref, a_ref, dskip_ref, gn_ref,
                o_ref, xext_ref, state_ref):
    cl = xbc_ref.shape[1]
    width = z_ref.shape[2]
    gw = width // SSM_GROUPS
    n = SSM_STATE

    @pl.when(pl.program_id(1) == 0)
    def _():
        xext_ref[0:HALO, :] = jnp.zeros((HALO, xext_ref.shape[1]), F32)
        state_ref[...] = jnp.zeros_like(state_ref)

    xext_ref[HALO:, :] = xbc_ref[0]
    taps = cw_ref.shape[0]
    xc = cb_ref[...] + sum(
        cw_ref[k:k + 1, :] * xext_ref[pl.ds(HALO - taps + 1 + k, cl), :] for k in range(taps))
    xext_ref[0:HALO, :] = xext_ref[cl:cl + HALO, :]
    xc = _silu(xc)
    xs = xc[:, :width]

    dt = _softplus(dt_ref[0] + dtb_ref[...])
    da = dt * a_ref[...]
    l_idx = lax.broadcasted_iota(jnp.int32, (cl, cl), 0)
    s_idx = lax.broadcasted_iota(jnp.int32, (cl, cl), 1)
    causal = l_idx >= s_idx
    tril = causal.astype(BF16)
    a_cs = sum(_dot(tril, piece) for piece in _split3(da))
    a_cs_t = a_cs.T

    lane = lax.broadcasted_iota(jnp.int32, (cl, LANES), 1)
    low = lane < SSM_HEAD_DIM

    def expand(cols):
        return jnp.concatenate(
            [jnp.where(low, jnp.broadcast_to(cols[:, 2 * i:2 * i + 1], (cl, LANES)),
                       jnp.broadcast_to(cols[:, 2 * i + 1:2 * i + 2], (cl, LANES)))
             for i in range(SSM_HEADS // 2)], axis=1)

    dt_x = expand(dt)
    acs_x = expand(a_cs)
    decay_in = jnp.exp(acs_x)
    decay_out = jnp.exp(acs_x[cl - 1:cl, :] - acs_x)
    xdt = xs * dt_x

    heads_per_group = SSM_HEADS // SSM_GROUPS
    ys = []
    for g in range(SSM_GROUPS):
        bm = xc[:, width + g * n:width + (g + 1) * n]
        cm = xc[:, width + SSM_GROUPS * n + g * n:width + SSM_GROUPS * n + (g + 1) * n].astype(BF16)
        bm_t = bm.T.astype(BF16)
        cb = _dot(cm, bm_t)
        cols = slice(g * gw, (g + 1) * gw)
        y_diag = []
        for pair in range(heads_per_group // 2):
            ms = []
            for h in range(g * heads_per_group + 2 * pair, g * heads_per_group + 2 * pair + 2):
                seg = a_cs[:, h:h + 1] - a_cs_t[h:h + 1, :]
                decay = jnp.where(causal, jnp.exp(jnp.where(causal, seg, 0.0)), 0.0)
                ms.append((cb * decay).astype(BF16))
            c0 = g * gw + pair * LANES
            x_pair = xdt[:, c0:c0 + LANES]
            x_bd = jnp.concatenate([jnp.where(low, x_pair, 0.0), jnp.where(low, 0.0, x_pair)], axis=0)
            y_diag.append(_dot(jnp.concatenate(ms, axis=1), x_bd.astype(BF16)))
        y_diag = jnp.concatenate(y_diag, axis=1)
        state = state_ref[g]
        y_off = _dot(cm, state.astype(BF16)) * decay_in[:, cols]
        state_ref[g] = decay_in[cl - 1:cl, cols] * state + _dot(
            bm_t, (xdt[:, cols] * decay_out[:, cols]).astype(BF16))
        ys.append(y_diag + y_off)
    y = jnp.concatenate(ys, axis=1) + dskip_ref[...] * xs
    y = y * _silu(z_ref[0])
    y = jnp.concatenate(
        [y[:, g * gw:(g + 1) * gw]
         * lax.rsqrt(jnp.mean(jnp.square(y[:, g * gw:(g + 1) * gw]), axis=-1, keepdims=True) + EPS)
         for g in range(SSM_GROUPS)], axis=1)
    o_ref[0] = (y * gn_ref[...]).astype(o_ref.dtype)


def _ssd(xbc, z, dt_raw, conv_w, conv_b, dt_bias, a_log, d_skip, ssm_norm):
    b, s, xw = xbc.shape
    width = z.shape[2]
    cl = SSM_CHUNK
    pad = lambda v: jnp.pad(v.astype(F32), (0, LANES - v.shape[0])).reshape(1, LANES)
    seq = lambda w: pl.BlockSpec((1, cl, w), lambda bi, ci: (bi, ci, 0))
    return pl.pallas_call(
        _ssd_kernel,
        grid=(b, s // cl),
        in_specs=[seq(xw), seq(width), seq(LANES), _const_spec(conv_w.shape), _const_spec((1, xw)),
                  _const_spec((1, LANES)), _const_spec((1, LANES)), _const_spec((1, width)),
                  _const_spec((1, width))],
        out_specs=seq(width),
        out_shape=jax.ShapeDtypeStruct((b, s, width), BF16),
        scratch_shapes=[pltpu.VMEM((cl + HALO, xw), F32),
                        pltpu.VMEM((SSM_GROUPS, SSM_STATE, width // SSM_GROUPS), F32)],
        compiler_params=_params("parallel", "arbitrary"),
        name="ssd",
    )(xbc, z, dt_raw, conv_w, conv_b.reshape(1, xw), pad(dt_bias), pad(-jnp.exp(a_log.astype(F32))),
      jnp.repeat(d_skip.astype(F32), SSM_HEAD_DIM).reshape(1, width), ssm_norm.reshape(1, width))


def _conformer_kernel(glu_ref, cw_ref, cb_ref, g_ref, b_ref, o_ref, uext_ref, shift_ref):
    ts, w = o_ref.shape[1], o_ref.shape[2]

    @pl.when(pl.program_id(1) == 0)
    def _():
        uext_ref[0:CONF_HALO, :] = jnp.zeros((CONF_HALO, w), F32)

    uext_ref[CONF_HALO:, :] = glu_ref[0, :, :w] * _sigmoid(glu_ref[0, :, w:])
    taps = cw_ref.shape[0]
    y = cb_ref[...]
    for phase in range(min(SUBLANES, taps)):
        ks = range(phase, taps, SUBLANES)
        span = ts + SUBLANES * (len(ks) - 1)
        shift_ref[0:span, :] = uext_ref[pl.ds(CONF_HALO - taps + 1 + phase, span), :]
        for a, k in enumerate(ks):
            y = y + cw_ref[k:k + 1, :] * shift_ref[SUBLANES * a:SUBLANES * a + ts, :]
    uext_ref[0:CONF_HALO, :] = uext_ref[ts:ts + CONF_HALO, :]
    yc = y - jnp.mean(y, axis=-1, keepdims=True)
    yn = yc * lax.rsqrt(jnp.mean(yc * yc, axis=-1, keepdims=True) + EPS)
    o_ref[0] = _silu(yn * g_ref[...] + b_ref[...]).astype(o_ref.dtype)


def _conformer(glu, conv_w, conv_b, ln_g, ln_b, ts):
    b, s, w2 = glu.shape
    w = w2 // 2
    return pl.pallas_call(
        _conformer_kernel,
        grid=(b, s // ts),
        in_specs=[pl.BlockSpec((1, ts, w2), lambda bi, si: (bi, si, 0)), _const_spec(conv_w.shape),
                  _const_spec((1, w)), _const_spec((1, w)), _const_spec((1, w))],
        out_specs=pl.BlockSpec((1, ts, w), lambda bi, si: (bi, si, 0)),
        out_shape=jax.ShapeDtypeStruct((b, s, w), BF16),
        scratch_shapes=[pltpu.VMEM((ts + CONF_HALO, w), F32), pltpu.VMEM((ts + CONF_HALO, w), F32)],
        compiler_params=_params("parallel", "arbitrary"),
        name="conformer",
    )(glu, conv_w, conv_b.reshape(1, w), ln_g.reshape(1, w), ln_b.reshape(1, w))


def _post_kernel(*refs, n_mix, hidden_chunk):
    h_ref, p_ref = refs[0], refs[1]
    y_refs = refs[2:2 + n_mix]
    wo_refs = refs[2 + n_mix:2 + 2 * n_mix]
    (g_post_ref, g_pre_ref, w1_ref, w2_ref, g_mlp_ref, wg_ref, wp_ref, g_ple_ref, o_ref) = refs[2 + 2 * n_mix:]

    m = sum(_dot(y_ref[...], wo_ref[...]) for y_ref, wo_ref in zip(y_refs, wo_refs))
    h = h_ref[...] + _rms(m, g_post_ref[...])

    hn = _rms(h, g_pre_ref[...]).astype(BF16)
    f = jnp.zeros_like(h)
    for c in range(0, w1_ref.shape[1], hidden_chunk):
        a = jnp.maximum(_dot(hn, w1_ref[:, c:c + hidden_chunk]), 0.0)
        f = f + _dot((a * a).astype(BF16), w2_ref[c:c + hidden_chunk, :])
    h = h + _rms(f, g_mlp_ref[...])

    gate = _sigmoid(_dot(h.astype(BF16), wg_ref[...]))
    emb = _dot(p_ref[...].astype(BF16), wp_ref[...])
    o_ref[...] = h + _rms(gate * emb, g_ple_ref[...])


def _post(h, p, ys, w_out, g_post, g_pre, w1, w2, g_mlp, wg, wp, g_ple, tm):
    t, d = h.shape
    wos, r0 = [], 0
    for y in ys:
        wos.append(w_out[r0:r0 + y.shape[1]].astype(BF16))
        r0 += y.shape[1]
    rows = lambda w: pl.BlockSpec((tm, w), lambda i: (i, 0))
    gains = [g.reshape(1, d).astype(F32) for g in (g_post, g_pre, g_mlp, g_ple)]
    return pl.pallas_call(
        functools.partial(_post_kernel, n_mix=len(ys), hidden_chunk=min(1024, w1.shape[1])),
        grid=(t // tm,),
        in_specs=[rows(d), rows(p.shape[1])] + [rows(y.shape[1]) for y in ys]
        + [_const_spec(w.shape) for w in wos]
        + [_const_spec((1, d)), _const_spec((1, d)), _const_spec(w1.shape), _const_spec(w2.shape),
           _const_spec((1, d)), _const_spec(wg.shape), _const_spec(wp.shape), _const_spec((1, d))],
        out_specs=rows(d),
        out_shape=jax.ShapeDtypeStruct((t, d), F32),
        compiler_params=_params("parallel"),
        name="post_mixer",
    )(h, p, *ys, *wos, gains[0], gains[1], w1.astype(BF16), w2.astype(BF16), gains[2],
      wg.astype(BF16), wp.astype(BF16), gains[3])


def _even_layer(h, b, s, w_in, conv_w, conv_b, ga_w, ga_b, gx_w, gx_b, lam, g_pre, tm, ts, tq):
    lw = lam.shape[0]
    sw = (w_in.shape[1] - 2 * lw) // 3
    edges = [0, lw, 2 * lw, 2 * lw + sw, 2 * lw + 2 * sw, 2 * lw + 3 * sw]
    ws = [w_in[:, a:c].astype(BF16) for a, c in zip(edges[:-1], edges[1:])]
    lru_x, gate, q, k, v = _norm_proj(h, g_pre, ws, [F32, F32, BF16, BF16, BF16], tm)
    seq = lambda a: a.reshape(b, s, a.shape[1])
    y_a = _lru(seq(lru_x), seq(gate), conv_w, conv_b, ga_w, ga_b, gx_w, gx_b, lam, ts)
    y_b = _stick_breaking(seq(q), seq(k), seq(v), tq)
    return [y_a.reshape(b * s, lw), y_b.reshape(b * s, sw)]


def _odd_layer(h, b, s, w_in, conv_w, conv_b, dt_bias, a_log, d_skip, ssm_norm,
               cm_conv_w, cm_conv_b, cm_ln_g, cm_ln_b, g_pre, tm, ts):
    width = ssm_norm.shape[0]
    xw = conv_w.shape[1]
    nh = dt_bias.shape[0]
    cw = cm_ln_g.shape[0]
    e1, e2, e3 = width, width + xw, width + xw + nh
    w_dt = jnp.pad(w_in[:, e2:e3], ((0, 0), (0, LANES - nh)))
    ws = [w.astype(BF16) for w in (w_in[:, :e1], w_in[:, e1:e2], w_dt, w_in[:, e3:])]
    z, xbc, dt_raw, glu = _norm_proj(h, g_pre, ws, [F32, F32, F32, F32], tm)
    seq = lambda a: a.reshape(b, s, a.shape[1])
    y_c = _ssd(seq(xbc), seq(z), seq(dt_raw), conv_w, conv_b, dt_bias, a_log, d_skip, ssm_norm)
    y_d = _conformer(seq(glu), cm_conv_w, cm_conv_b, cm_ln_g, cm_ln_b, ts)
    return [y_c.reshape(b * s, width), y_d.reshape(b * s, cw)]


def kernel(x, p, ev_w_in, ev_lru_conv_w, ev_lru_conv_b, ev_lru_gate_a_w, ev_lru_gate_a_b, ev_lru_gate_x_w, ev_lru_gate_x_b, ev_lru_lambda, ev_w_out, od_w_in, od_ssm_conv_w, od_ssm_conv_b, od_ssm_dt_bias, od_ssm_a_log, od_ssm_d, od_ssm_norm, od_cm_conv_w, od_cm_conv_b, od_cm_ln_g, od_cm_ln_b, od_w_out, norm_mix_pre, norm_mix_post, norm_mlp_pre, norm_mlp_post, norm_ple, mlp_w1, mlp_w2, ple_w_proj, ple_w_gate):
    b, s, d = x.shape
    depth = p.shape[0]
    t = b * s
    tm = min(512, t)
    ts = min(256, s)
    tq = min(512, s)
    h = x.reshape(t, d)
    for i in range(depth):
        j = i // 2
        if i % 2 == 0:
            ys = _even_layer(h, b, s, ev_w_in[j], ev_lru_conv_w[j], ev_lru_conv_b[j], ev_lru_gate_a_w[j],
                             ev_lru_gate_a_b[j], ev_lru_gate_x_w[j], ev_lru_gate_x_b[j], ev_lru_lambda[j],
                             norm_mix_pre[i], tm, ts, tq)
            w_out = ev_w_out[j]
        else:
            ys = _odd_layer(h, b, s, od_w_in[j], od_ssm_conv_w[j], od_ssm_conv_b[j], od_ssm_dt_bias[j],
                            od_ssm_a_log[j], od_ssm_d[j], od_ssm_norm[j], od_cm_conv_w[j], od_cm_conv_b[j],
                            od_cm_ln_g[j], od_cm_ln_b[j], norm_mix_pre[i], tm, ts)
            w_out = od_w_out[j]
        h = _post(h, p[i].reshape(t, p.shape[3]), ys, w_out, norm_mix_post[i], norm_mlp_pre[i],
                  mlp_w1[i], mlp_w2[i], norm_mlp_post[i], ple_w_gate[i], ple_w_proj[i], norm_ple[i], tm)
    return h.reshape(b, s, d)
```

```python
import functools
import math

import jax
import jax.numpy as jnp
from jax import lax
from jax.experimental import pallas as pl
from jax.experimental.pallas import tpu as pltpu

F32 = jnp.float32
BF16 = jnp.bfloat16

EPS = 1e-6
LOG2E = 1.4426950408889634
LRU_HEADS = 8
LRU_C = 8.0
SB_HEAD_DIM = 64
SSM_HEADS = 16
SSM_HEAD_DIM = 64
SSM_GROUPS = 2
SSM_STATE = 128
SSM_CHUNK = 128
SB_KEYS = 256
SB_EXIT_LOG2 = 160.0

LANES = 128
SUBLANES = 8
HALO = 8
CONF_HALO = 32
VMEM_LIMIT = 56 * 1024 * 1024


def _params(*sem):
    return pltpu.CompilerParams(dimension_semantics=sem, vmem_limit_bytes=VMEM_LIMIT)


def _const_spec(shape):
    zeros = (0,) * len(shape)
    return pl.BlockSpec(shape, lambda *_: zeros, pipeline_mode=pl.Buffered(1))


def _rms(x, g):
    return x * lax.rsqrt(jnp.mean(x * x, axis=-1, keepdims=True) + EPS) * g


def _dot(a, b):
    return jnp.dot(a, b, preferred_element_type=F32)


def _split3(x):
    h1 = x.astype(BF16)
    r1 = x - h1.astype(F32)
    h2 = r1.astype(BF16)
    h3 = (r1 - h2.astype(F32)).astype(BF16)
    return h1, h2, h3


def _softplus(x):
    return jnp.maximum(x, 0.0) + jnp.log1p(jnp.exp(-jnp.abs(x)))


def _sigmoid(x):
    return 1.0 / (1.0 + jnp.exp(-x))


def _silu(x):
    return x * _sigmoid(x)


def _norm_proj_kernel(h_ref, g_ref, *refs):
    n = len(refs) // 2
    hn = _rms(h_ref[...], g_ref[...]).astype(BF16)
    for w_ref, o_ref in zip(refs[:n], refs[n:]):
        o_ref[...] = _dot(hn, w_ref[...]).astype(o_ref.dtype)


def _norm_proj(h, g, ws, dtypes, tm):
    t, d = h.shape
    return pl.pallas_call(
        _norm_proj_kernel,
        grid=(t // tm,),
        in_specs=[pl.BlockSpec((tm, d), lambda i: (i, 0)), _const_spec((1, d))]
        + [_const_spec(w.shape) for w in ws],
        out_specs=[pl.BlockSpec((tm, w.shape[1]), lambda i: (i, 0)) for w in ws],
        out_shape=[jax.ShapeDtypeStruct((t, w.shape[1]), dt) for w, dt in zip(ws, dtypes)],
        compiler_params=_params("parallel"),
        name="norm_proj",
    )(h, g.reshape(1, d), *ws)


def _lru_kernel(x_ref, gate_ref, cw_ref, cb_ref, wa_ref, ba_ref, wx_ref, bx_ref, lam_ref,
                o_ref, xext_ref, h_ref):
    ts, w = x_ref.shape[1], x_ref.shape[2]

    @pl.when(pl.program_id(1) == 0)
    def _():
        xext_ref[0:HALO, :] = jnp.zeros((HALO, w), F32)
        h_ref[...] = jnp.zeros_like(h_ref)

    xext_ref[HALO:, :] = x_ref[0]
    taps = cw_ref.shape[0]
    xc = cb_ref[...] + sum(
        cw_ref[k:k + 1, :] * xext_ref[pl.ds(HALO - taps + 1 + k, ts), :] for k in range(taps))
    xext_ref[0:HALO, :] = xext_ref[ts:ts + HALO, :]

    xb = xc.astype(BF16)
    r = _sigmoid(_dot(xb, wa_ref[...]) + ba_ref[...])
    i = _sigmoid(_dot(xb, wx_ref[...]) + bx_ref[...])
    log_a = (-LRU_C) * r * _softplus(-lam_ref[...])
    a = jnp.exp(log_a)
    u = jnp.sqrt(-jnp.tanh(log_a) * (a * a + 1.0)) * (i * xc)

    row = lax.broadcasted_iota(jnp.int32, (ts, w), 0)
    d = 1
    while d < ts:
        keep = row >= d
        a_prev = jnp.where(keep, pltpu.roll(a, d, 0), 1.0)
        u_prev = jnp.where(keep, pltpu.roll(u, d, 0), 0.0)
        u = a * u_prev + u
        a = a * a_prev
        d *= 2
    hs = u + a * h_ref[...]
    h_ref[...] = hs[ts - 1:ts, :]
    o_ref[0] = (hs * jax.nn.gelu(gate_ref[0])).astype(o_ref.dtype)


def _block_diag(w):
    nh, ni, nj = w.shape
    eye = jnp.eye(nh, dtype=w.dtype)
    return (w[:, :, None, :] * eye[:, None, :, None]).reshape(nh * ni, nh * nj)


def _lru(x, gate, conv_w, conv_b, ga_w, ga_b, gx_w, gx_b, lam, ts):
    b, s, w = x.shape
    row = lambda v: v.reshape(1, w).astype(F32)
    seq_spec = pl.BlockSpec((1, ts, w), lambda bi, si: (bi, si, 0))
    return pl.pallas_call(
        _lru_kernel,
        grid=(b, s // ts),
        in_specs=[seq_spec, seq_spec, _const_spec(conv_w.shape), _const_spec((1, w)),
                  _const_spec((w, w)), _const_spec((1, w)), _const_spec((w, w)), _const_spec((1, w)),
                  _const_spec((1, w))],
        out_specs=seq_spec,
        out_shape=jax.ShapeDtypeStruct((b, s, w), BF16),
        scratch_shapes=[pltpu.VMEM((ts + HALO, w), F32), pltpu.VMEM((1, w), F32)],
        compiler_params=_params("parallel", "arbitrary"),
        name="rg_lru",
    )(x, gate, conv_w, row(conv_b), _block_diag(ga_w).astype(BF16), row(ga_b),
      _block_diag(gx_w).astype(BF16), row(gx_b), row(lam))


def _sb_kernel(q_ref, k_ref, v_ref, cum_ref, o_ref, acc_ref, skip_ref, *z_refs, scale):
    tq = q_ref.shape[1]
    tk = SB_KEYS
    n_heads = LANES // SB_HEAD_DIM
    n_blocks = tq // tk
    qb = pl.program_id(2)
    lane = lax.broadcasted_iota(jnp.int32, (1, LANES), 1)
    below_diag = lax.broadcasted_iota(jnp.int32, (tk, tk), 1) < lax.broadcasted_iota(jnp.int32, (tk, tk), 0)
    q = q_ref[0] * scale
    qs = [jnp.where((lane // SB_HEAD_DIM) == h, q, jnp.zeros_like(q)) for h in range(n_heads)]
    cum = cum_ref[...]
    all_rows = slice(0, tq)

    def scores(kb, z_ref, rows):
        k_blk = k_ref[0, pl.ds(kb * tk, tk), :]
        for h in range(n_heads):
            z_ref[h, rows] = lax.dot_general(
                qs[h][rows], k_blk, (((1,), (1,)), ((), ())), preferred_element_type=F32) * LOG2E

    def block(kb, z_ref, rows, strict):
        v_blk = v_ref[0, pl.ds(kb * tk, tk), :]
        wts, vs = [], []
        for h in range(n_heads):
            z = z_ref[h, rows]
            minus_abs = lax.bitcast_convert_type(
                lax.bitcast_convert_type(z, jnp.uint32) | jnp.uint32(0x80000000), F32)
            sp = jnp.maximum(z, 0.0) + jnp.log2(1.0 + jnp.exp2(minus_abs))
            log_beta = z - sp
            if strict is not None:
                sp = jnp.where(strict, sp, 0.0)
            sp = sp.astype(BF16)
            after = _dot(sp, cum)
            skipped = skip_ref[h, rows]
            w = jnp.exp2(log_beta - after - skipped)
            skip_ref[h, rows] = skipped + (after[:, 0:1] + sp[:, 0:1].astype(F32))
            if strict is not None:
                w = jnp.where(strict, w, 0.0)
            wts.append(w.astype(BF16))
            vs.append(jnp.where((lane // SB_HEAD_DIM) == h, v_blk, jnp.zeros_like(v_blk)))
        acc_ref[rows] += _dot(jnp.concatenate(wts, axis=1), jnp.concatenate(vs, axis=0))

    acc_ref[...] = jnp.zeros_like(acc_ref)
    skip_ref[...] = jnp.zeros_like(skip_ref)
    tiles = [(slice(r * tk, (r + 1) * tk), d, d == r) for r in range(n_blocks) for d in reversed(range(r + 1))]
    for rows, d, _ in tiles:
        scores(qb * n_blocks + d, z_refs[d], rows)
    for rows, d, on_diag in tiles:
        block(qb * n_blocks + d, z_refs[d], rows, below_diag if on_diag else None)

    def key_block(i, d):
        return jnp.maximum((qb - i) * n_blocks - 1 - d, 0)

    def trip(c):
        i, _ = c
        for d in range(n_blocks):
            nxt = (d + 1) % n_blocks
            scores(key_block(i + (d + 1) // n_blocks, nxt), z_refs[nxt], all_rows)
            block(key_block(i, d), z_refs[d], all_rows, None)
            if d == 0:
                least = jnp.min(skip_ref[...])
        return i + 1, (least >= SB_EXIT_LOG2).astype(jnp.int32)

    scores(key_block(0, 0), z_refs[0], all_rows)
    lax.while_loop(lambda c: jnp.logical_and(c[0] < qb, c[1] == 0), trip, (jnp.int32(0), jnp.int32(0)))
    o_ref[0] = acc_ref[...].astype(o_ref.dtype)


def _stick_breaking(q, k, v, tq):
    b, s, w = q.shape
    n_heads = LANES // SB_HEAD_DIM
    cum = (jnp.arange(SB_KEYS)[:, None] > jnp.arange(SB_KEYS)[None, :]).astype(BF16)
    kv_spec = pl.BlockSpec((1, s, LANES), lambda bi, hi, qi: (bi, 0, hi))
    q_spec = pl.BlockSpec((1, tq, LANES), lambda bi, hi, qi: (bi, qi, hi))
    return pl.pallas_call(
        functools.partial(_sb_kernel, scale=1.0 / math.sqrt(SB_HEAD_DIM)),
        grid=(b, w // LANES, s // tq),
        in_specs=[q_spec, kv_spec, kv_spec, _const_spec(cum.shape)],
        out_specs=q_spec,
        out_shape=jax.ShapeDtypeStruct((b, s, w), BF16),
        scratch_shapes=[pltpu.VMEM((tq, LANES), F32), pltpu.VMEM((n_heads, tq, 1), F32)]
        + [pltpu.VMEM((n_heads, tq, SB_KEYS), F32) for _ in range(tq // SB_KEYS)],
        compiler_params=_params("parallel", "parallel", "arbitrary"),
        name="stick_breaking",
    )(q, k, v, cum)


def _ssd_kernel(xbc_ref, z_ref, dt_ref, cw_ref, cb_ref, dtb_ref, a_ref, dskip_ref, gn_ref,
                o_ref, xext_ref, state_ref):
    cl = xbc_ref.shape[1]
    width = z_ref.shape[2]
    gw = width // SSM_GROUPS
    n = SSM_STATE

    @pl.when(pl.program_id(1) == 0)
    def _():
        xext_ref[0:HALO, :] = jnp.zeros((HALO, xext_ref.shape[1]), F32)
        state_ref[...] = jnp.zeros_like(state_ref)

    xext_ref[HALO:, :] = xbc_ref[0]
    taps = cw_ref.shape[0]
    xc = cb_ref[...] + sum(
        cw_ref[k:k + 1, :] * xext_ref[pl.ds(HALO - taps + 1 + k, cl), :] for k in range(taps))
    xext_ref[0:HALO, :] = xext_ref[cl:cl + HALO, :]
    xc = _silu(xc)
    xs = xc[:, :width]

    dt = _softplus(dt_ref[0] + dtb_ref[...])
    da = dt * a_ref[...]
    l_idx = lax.broadcasted_iota(jnp.int32, (cl, cl), 0)
    s_idx = lax.broadcasted_iota(jnp.int32, (cl, cl), 1)
    causal = l_idx >= s_idx
    tril = causal.astype(BF16)
    a_cs = sum(_dot(tril, piece) for piece in _split3(da))
    a_cs_t = a_cs.T

    lane = lax.broadcasted_iota(jnp.int32, (cl, LANES), 1)
    low = lane < SSM_HEAD_DIM

    def expand(cols):
        return jnp.concatenate(
            [jnp.where(low, jnp.broadcast_to(cols[:, 2 * i:2 * i + 1], (cl, LANES)),
                       jnp.broadcast_to(cols[:, 2 * i + 1:2 * i + 2], (cl, LANES)))
             for i in range(SSM_HEADS // 2)], axis=1)

    dt_x = expand(dt)
    acs_x = expand(a_cs)
    decay_in = jnp.exp(acs_x)
    decay_out = jnp.exp(acs_x[cl - 1:cl, :] - acs_x)
    xdt = xs * dt_x

    heads_per_group = SSM_HEADS // SSM_GROUPS
    ys = []
    for g in range(SSM_GROUPS):
        bm = xc[:, width + g * n:width + (g + 1) * n]
        cm = xc[:, width + SSM_GROUPS * n + g * n:width + SSM_GROUPS * n + (g + 1) * n].astype(BF16)
        bm_t = bm.T.astype(BF16)
        cb = _dot(cm, bm_t)
        cols = slice(g * gw, (g + 1) * gw)
        y_diag = []
        for pair in range(heads_per_group // 2):
            ms = []
            for h in range(g * heads_per_group + 2 * pair, g * heads_per_group + 2 * pair + 2):
                seg = a_cs[:, h:h + 1] - a_cs_t[h:h + 1, :]
                decay = jnp.where(causal, jnp.exp(jnp.where(causal, seg, 0.0)), 0.0)
                ms.append((cb * decay).astype(BF16))
            c0 = g * gw + pair * LANES
            x_pair = xdt[:, c0:c0 + LANES]
            x_bd = jnp.concatenate([jnp.where(low, x_pair, 0.0), jnp.where(low, 0.0, x_pair)], axis=0)
            y_diag.append(_dot(jnp.concatenate(ms, axis=1), x_bd.astype(BF16)))
        y_diag = jnp.concatenate(y_diag, axis=1)
        state = state_ref[g]
        y_off = _dot(cm, state.astype(BF16)) * decay_in[:, cols]
        state_ref[g] = decay_in[cl - 1:cl, cols] * state + _dot(
            bm_t, (xdt[:, cols] * decay_out[:, cols]).astype(BF16))
        ys.append(y_diag + y_off)
    y = jnp.concatenate(ys, axis=1) + dskip_ref[...] * xs
    y = y * _silu(z_ref[0])
    y = jnp.concatenate(
        [y[:, g * gw:(g + 1) * gw]
         * lax.rsqrt(jnp.mean(jnp.square(y[:, g * gw:(g + 1) * gw]), axis=-1, keepdims=True) + EPS)
         for g in range(SSM_GROUPS)], axis=1)
    o_ref[0] = (y * gn_ref[...]).astype(o_ref.dtype)


def _ssd(xbc, z, dt_raw, conv_w, conv_b, dt_bias, a_log, d_skip, ssm_norm):
    b, s, xw = xbc.shape
    width = z.shape[2]
    cl = SSM_CHUNK
    pad = lambda v: jnp.pad(v.astype(F32), (0, LANES - v.shape[0])).reshape(1, LANES)
    seq = lambda w: pl.BlockSpec((1, cl, w), lambda bi, ci: (bi, ci, 0))
    return pl.pallas_call(
        _ssd_kernel,
        grid=(b, s // cl),
        in_specs=[seq(xw), seq(width), seq(LANES), _const_spec(conv_w.shape), _const_spec((1, xw)),
                  _const_spec((1, LANES)), _const_spec((1, LANES)), _const_spec((1, width)),
                  _const_spec((1, width))],
        out_specs=seq(width),
        out_shape=jax.ShapeDtypeStruct((b, s, width), BF16),
        scratch_shapes=[pltpu.VMEM((cl + HALO, xw), F32),
                        pltpu.VMEM((SSM_GROUPS, SSM_STATE, width // SSM_GROUPS), F32)],
        compiler_params=_params("parallel", "arbitrary"),
        name="ssd",
    )(xbc, z, dt_raw, conv_w, conv_b.reshape(1, xw), pad(dt_bias), pad(-jnp.exp(a_log.astype(F32))),
      jnp.repeat(d_skip.astype(F32), SSM_HEAD_DIM).reshape(1, width), ssm_norm.reshape(1, width))


def _conformer_kernel(glu_ref, cw_ref, cb_ref, g_ref, b_ref, o_ref, uext_ref, shift_ref):
    ts, w = o_ref.shape[1], o_ref.shape[2]

    @pl.when(pl.program_id(1) == 0)
    def _():
        uext_ref[0:CONF_HALO, :] = jnp.zeros((CONF_HALO, w), F32)

    uext_ref[CONF_HALO:, :] = glu_ref[0, :, :w] * _sigmoid(glu_ref[0, :, w:])
    taps = cw_ref.shape[0]
    y = cb_ref[...]
    for phase in range(min(SUBLANES, taps)):
        ks = range(phase, taps, SUBLANES)
        span = ts + SUBLANES * (len(ks) - 1)
        shift_ref[0:span, :] = uext_ref[pl.ds(CONF_HALO - taps + 1 + phase, span), :]
        for a, k in enumerate(ks):
            y = y + cw_ref[k:k + 1, :] * shift_ref[SUBLANES * a:SUBLANES * a + ts, :]
    uext_ref[0:CONF_HALO, :] = uext_ref[ts:ts + CONF_HALO, :]
    yc = y - jnp.mean(y, axis=-1, keepdims=True)
    yn = yc * lax.rsqrt(jnp.mean(yc * yc, axis=-1, keepdims=True) + EPS)
    o_ref[0] = _silu(yn * g_ref[...] + b_ref[...]).astype(o_ref.dtype)


def _conformer(glu, conv_w, conv_b, ln_g, ln_b, ts):
    b, s, w2 = glu.shape
    w = w2 // 2
    return pl.pallas_call(
        _conformer_kernel,
        grid=(b, s // ts),
        in_specs=[pl.BlockSpec((1, ts, w2), lambda bi, si: (bi, si, 0)), _const_spec(conv_w.shape),
                  _const_spec((1, w)), _const_spec((1, w)), _const_spec((1, w))],
        out_specs=pl.BlockSpec((1, ts, w), lambda bi, si: (bi, si, 0)),
        out_shape=jax.ShapeDtypeStruct((b, s, w), BF16),
        scratch_shapes=[pltpu.VMEM((ts + CONF_HALO, w), F32), pltpu.VMEM((ts + CONF_HALO, w), F32)],
        compiler_params=_params("parallel", "arbitrary"),
        name="conformer",
    )(glu, conv_w, conv_b.reshape(1, w), ln_g.reshape(1, w), ln_b.reshape(1, w))


def _post_kernel(*refs, n_mix, hidden_chunk):
    h_ref, p_ref = refs[0], refs[1]
    y_refs = refs[2:2 + n_mix]
    wo_refs = refs[2 + n_mix:2 + 2 * n_mix]
    (g_post_ref, g_pre_ref, w1_ref, w2_ref, g_mlp_ref, wg_ref, wp_ref, g_ple_ref, o_ref) = refs[2 + 2 * n_mix:]

    m = sum(_dot(y_ref[...], wo_ref[...]) for y_ref, wo_ref in zip(y_refs, wo_refs))
    h = h_ref[...] + _rms(m, g_post_ref[...])

    hn = _rms(h, g_pre_ref[...]).astype(BF16)
    f = jnp.zeros_like(h)
    for c in range(0, w1_ref.shape[1], hidden_chunk):
        a = jnp.maximum(_dot(hn, w1_ref[:, c:c + hidden_chunk]), 0.0)
        f = f + _dot((a * a).astype(BF16), w2_ref[c:c + hidden_chunk, :])
    h = h + _rms(f, g_mlp_ref[...])

    gate = _sigmoid(_dot(h.astype(BF16), wg_ref[...]))
    emb = _dot(p_ref[...].astype(BF16), wp_ref[...])
    o_ref[...] = h + _rms(gate * emb, g_ple_ref[...])


def _post(h, p, ys, w_out, g_post, g_pre, w1, w2, g_mlp, wg, wp, g_ple, tm):
    t, d = h.shape
    wos, r0 = [], 0
    for y in ys:
        wos.append(w_out[r0:r0 + y.shape[1]].astype(BF16))
        r0 += y.shape[1]
    rows = lambda w: pl.BlockSpec((tm, w), lambda i: (i, 0))
    gains = [g.reshape(1, d).astype(F32) for g in (g_post, g_pre, g_mlp, g_ple)]
    return pl.pallas_call(
        functools.partial(_post_kernel, n_mix=len(ys), hidden_chunk=min(1024, w1.shape[1])),
        grid=(t // tm,),
        in_specs=[rows(d), rows(p.shape[1])] + [rows(y.shape[1]) for y in ys]
        + [_const_spec(w.shape) for w in wos]
        + [_const_spec((1, d)), _const_spec((1, d)), _const_spec(w1.shape), _const_spec(w2.shape),
           _const_spec((1, d)), _const_spec(wg.shape), _const_spec(wp.shape), _const_spec((1, d))],
        out_specs=rows(d),
        out_shape=jax.ShapeDtypeStruct((t, d), F32),
        compiler_params=_params("parallel"),
        name="post_mixer",
    )(h, p, *ys, *wos, gains[0], gains[1], w1.astype(BF16), w2.astype(BF16), gains[2],
      wg.astype(BF16), wp.astype(BF16), gains[3])


def _even_layer(h, b, s, w_in, conv_w, conv_b, ga_w, ga_b, gx_w, gx_b, lam, g_pre, tm, ts, tq):
    lw = lam.shape[0]
    sw = (w_in.shape[1] - 2 * lw) // 3
    edges = [0, lw, 2 * lw, 2 * lw + sw, 2 * lw + 2 * sw, 2 * lw + 3 * sw]
    ws = [w_in[:, a:c].astype(BF16) for a, c in zip(edges[:-1], edges[1:])]
    lru_x, gate, q, k, v = _norm_proj(h, g_pre, ws, [F32, F32, BF16, BF16, BF16], tm)
    seq = lambda a: a.reshape(b, s, a.shape[1])
    y_a = _lru(seq(lru_x), seq(gate), conv_w, conv_b, ga_w, ga_b, gx_w, gx_b, lam, ts)
    y_b = _stick_breaking(seq(q), seq(k), seq(v), tq)
    return [y_a.reshape(b * s, lw), y_b.reshape(b * s, sw)]


def _odd_layer(h, b, s, w_in, conv_w, conv_b, dt_bias, a_log, d_skip, ssm_norm,
               cm_conv_w, cm_conv_b, cm_ln_g, cm_ln_b, g_pre, tm, ts):
    width = ssm_norm.shape[0]
    xw = conv_w.shape[1]
    nh = dt_bias.shape[0]
    cw = cm_ln_g.shape[0]
    e1, e2, e3 = width, width + xw, width + xw + nh
    w_dt = jnp.pad(w_in[:, e2:e3], ((0, 0), (0, LANES - nh)))
    ws = [w.astype(BF16) for w in (w_in[:, :e1], w_in[:, e1:e2], w_dt, w_in[:, e3:])]
    z, xbc, dt_raw, glu = _norm_proj(h, g_pre, ws, [F32, F32, F32, F32], tm)
    seq = lambda a: a.reshape(b, s, a.shape[1])
    y_c = _ssd(seq(xbc), seq(z), seq(dt_raw), conv_w, conv_b, dt_bias, a_log, d_skip, ssm_norm)
    y_d = _conformer(seq(glu), cm_conv_w, cm_conv_b, cm_ln_g, cm_ln_b, ts)
    return [y_c.reshape(b * s, width), y_d.reshape(b * s, cw)]


def kernel(x, p, ev_w_in, ev_lru_conv_w, ev_lru_conv_b, ev_lru_gate_a_w, ev_lru_gate_a_b, ev_lru_gate_x_w, ev_lru_gate_x_b, ev_lru_lambda, ev_w_out, od_w_in, od_ssm_conv_w, od_ssm_conv_b, od_ssm_dt_bias, od_ssm_a_log, od_ssm_d, od_ssm_norm, od_cm_conv_w, od_cm_conv_b, od_cm_ln_g, od_cm_ln_b, od_w_out, norm_mix_pre, norm_mix_post, norm_mlp_pre, norm_mlp_post, norm_ple, mlp_w1, mlp_w2, ple_w_proj, ple_w_gate):
    b, s, d = x.shape
    depth = p.shape[0]
    t = b * s
    tm = min(512, t)
    ts = min(256, s)
    tq = min(512, s)
    h = x.reshape(t, d)
    for i in range(depth):
        j = i // 2
        if i % 2 == 0:
            ys = _even_layer(h, b, s, ev_w_in[j], ev_lru_conv_w[j], ev_lru_conv_b[j], ev_lru_gate_a_w[j],
                             ev_lru_gate_a_b[j], ev_lru_gate_x_w[j], ev_lru_gate_x_b[j], ev_lru_lambda[j],
                             norm_mix_pre[i], tm, ts, tq)
            w_out = ev_w_out[j]
        else:
            ys = _odd_layer(h, b, s, od_w_in[j], od_ssm_conv_w[j], od_ssm_conv_b[j], od_ssm_dt_bias[j],
                            od_ssm_a_log[j], od_ssm_d[j], od_ssm_norm[j], od_cm_conv_w[j], od_cm_conv_b[j],
                            od_cm_ln_g[j], od_cm_ln_b[j], norm_mix_pre[i], tm, ts)
            w_out = od_w_out[j]
        h = _post(h, p[i].reshape(t, p.shape[3]), ys, w_out, norm_mix_post[i], norm_mlp_pre[i],
                  mlp_w1[i], mlp_w2[i], norm_mlp_post[i], ple_w_gate[i], ple_w_proj[i], norm_ple[i], tm)
    return h.reshape(b, s, d)
```

```python
import functools
import math

import jax
import jax.numpy as jnp
from jax import lax
from jax.experimental import pallas as pl
from jax.experimental.pallas import tpu as pltpu

F32 = jnp.float32
BF16 = jnp.bfloat16

EPS = 1e-6
LOG2E = 1.4426950408889634
LRU_HEADS = 8
LRU_C = 8.0
SB_HEAD_DIM = 64
SSM_HEADS = 16
SSM_HEAD_DIM = 64
SSM_GROUPS = 2
SSM_STATE = 128
SSM_CHUNK = 128
SB_KEYS = 256
SB_EXIT_LOG2 = 160.0

LANES = 128
SUBLANES = 8
HALO = 8
CONF_HALO = 32
VMEM_LIMIT = 56 * 1024 * 1024


def _params(*sem):
    return pltpu.CompilerParams(dimension_semantics=sem, vmem_limit_bytes=VMEM_LIMIT)


def _const_spec(shape):
    zeros = (0,) * len(shape)
    return pl.BlockSpec(shape, lambda *_: zeros, pipeline_mode=pl.Buffered(1))


def _rms(x, g):
    return x * lax.rsqrt(jnp.mean(x * x, axis=-1, keepdims=True) + EPS) * g


def _dot(a, b):
    return jnp.dot(a, b, preferred_element_type=F32)


def _split3(x):
    h1 = x.astype(BF16)
    r1 = x - h1.astype(F32)
    h2 = r1.astype(BF16)
    h3 = (r1 - h2.astype(F32)).astype(BF16)
    return h1, h2, h3


def _softplus(x):
    return jnp.maximum(x, 0.0) + jnp.log1p(jnp.exp(-jnp.abs(x)))


def _sigmoid(x):
    return 1.0 / (1.0 + jnp.exp(-x))


def _silu(x):
    return x * _sigmoid(x)


def _norm_proj_kernel(h_ref, g_ref, *refs):
    n = len(refs) // 2
    hn = _rms(h_ref[...], g_ref[...]).astype(BF16)
    for w_ref, o_ref in zip(refs[:n], refs[n:]):
        o_ref[...] = _dot(hn, w_ref[...]).astype(o_ref.dtype)


def _norm_proj(h, g, ws, dtypes, tm):
    t, d = h.shape
    return pl.pallas_call(
        _norm_proj_kernel,
        grid=(t // tm,),
        in_specs=[pl.BlockSpec((tm, d), lambda i: (i, 0)), _const_spec((1, d))]
        + [_const_spec(w.shape) for w in ws],
        out_specs=[pl.BlockSpec((tm, w.shape[1]), lambda i: (i, 0)) for w in ws],
        out_shape=[jax.ShapeDtypeStruct((t, w.shape[1]), dt) for w, dt in zip(ws, dtypes)],
        compiler_params=_params("parallel"),
        name="norm_proj",
    )(h, g.reshape(1, d), *ws)


def _lru_kernel(x_ref, gate_ref, cw_ref, cb_ref, wa_ref, ba_ref, wx_ref, bx_ref, lam_ref,
                o_ref, xext_ref, h_ref):
    ts, w = x_ref.shape[1], x_ref.shape[2]

    @pl.when(pl.program_id(1) == 0)
    def _():
        xext_ref[0:HALO, :] = jnp.zeros((HALO, w), F32)
        h_ref[...] = jnp.zeros_like(h_ref)

    xext_ref[HALO:, :] = x_ref[0]
    taps = cw_ref.shape[0]
    xc = cb_ref[...] + sum(
        cw_ref[k:k + 1, :] * xext_ref[pl.ds(HALO - taps + 1 + k, ts), :] for k in range(taps))
    xext_ref[0:HALO, :] = xext_ref[ts:ts + HALO, :]

    xb = xc.astype(BF16)
    r = _sigmoid(_dot(xb, wa_ref[...]) + ba_ref[...])
    i = _sigmoid(_dot(xb, wx_ref[...]) + bx_ref[...])
    log_a = (-LRU_C) * r * _softplus(-lam_ref[...])
    a = jnp.exp(log_a)
    u = jnp.sqrt(-jnp.tanh(log_a) * (a * a + 1.0)) * (i * xc)

    row = lax.broadcasted_iota(jnp.int32, (ts, w), 0)
    d = 1
    while d < ts:
        keep = row >= d
        a_prev = jnp.where(keep, pltpu.roll(a, d, 0), 1.0)
        u_prev = jnp.where(keep, pltpu.roll(u, d, 0), 0.0)
        u = a * u_prev + u
        a = a * a_prev
        d *= 2
    hs = u + a * h_ref[...]
    h_ref[...] = hs[ts - 1:ts, :]
    o_ref[0] = (hs * jax.nn.gelu(gate_ref[0])).astype(o_ref.dtype)


def _block_diag(w):
    nh, ni, nj = w.shape
    eye = jnp.eye(nh, dtype=w.dtype)
    return (w[:, :, None, :] * eye[:, None, :, None]).reshape(nh * ni, nh * nj)


def _lru(x, gate, conv_w, conv_b, ga_w, ga_b, gx_w, gx_b, lam, ts):
    b, s, w = x.shape
    row = lambda v: v.reshape(1, w).astype(F32)
    seq_spec = pl.BlockSpec((1, ts, w), lambda bi, si: (bi, si, 0))
    return pl.pallas_call(
        _lru_kernel,
        grid=(b, s // ts),
        in_specs=[seq_spec, seq_spec, _const_spec(conv_w.shape), _const_spec((1, w)),
                  _const_spec((w, w)), _const_spec((1, w)), _const_spec((w, w)), _const_spec((1, w)),
                  _const_spec((1, w))],
        out_specs=seq_spec,
        out_shape=jax.ShapeDtypeStruct((b, s, w), BF16),
        scratch_shapes=[pltpu.VMEM((ts + HALO, w), F32), pltpu.VMEM((1, w), F32)],
        compiler_params=_params("parallel", "arbitrary"),
        name="rg_lru",
    )(x, gate, conv_w, row(conv_b), _block_diag(ga_w).astype(BF16), row(ga_b),
      _block_diag(gx_w).astype(BF16), row(gx_b), row(lam))


def _sb_kernel(q_ref, k_ref, v_ref, cum_ref, o_ref, acc_ref, skip_ref, *z_refs, scale):
    tq = q_ref.shape[1]
    tk = SB_KEYS
    n_heads = LANES // SB_HEAD_DIM
    n_blocks = tq // tk
    qb = pl.program_id(2)
    lane = lax.broadcasted_iota(jnp.int32, (1, LANES), 1)
    below_diag = lax.broadcasted_iota(jnp.int32, (tk, tk), 1) < lax.broadcasted_iota(jnp.int32, (tk, tk), 0)
    q = q_ref[0] * scale
    qs = [jnp.where((lane // SB_HEAD_DIM) == h, q, jnp.zeros_like(q)) for h in range(n_heads)]
    cum = cum_ref[...]
    all_rows = slice(0, tq)

    def scores(kb, z_ref, rows):
        k_blk = k_ref[0, pl.ds(kb * tk, tk), :]
        for h in range(n_heads):
            z_ref[h, rows] = lax.dot_general(
                qs[h][rows], k_blk, (((1,), (1,)), ((), ())), preferred_element_type=F32) * LOG2E

    def block(kb, z_ref, rows, strict):
        v_blk = v_ref[0, pl.ds(kb * tk, tk), :]
        wts, vs = [], []
        for h in range(n_heads):
            z = z_ref[h, rows]
            minus_abs = lax.bitcast_convert_type(
                lax.bitcast_convert_type(z, jnp.uint32) | jnp.uint32(0x80000000), F32)
            sp = jnp.maximum(z, 0.0) + jnp.log2(1.0 + jnp.exp2(minus_abs))
            log_beta = z - sp
            if strict is not None:
                sp = jnp.where(strict, sp, 0.0)
            sp = sp.astype(BF16)
            after = _dot(sp, cum)
            skipped = skip_ref[h, rows]
            w = jnp.exp2(log_beta - after - skipped)
            skip_ref[h, rows] = skipped + (after[:, 0:1] + sp[:, 0:1].astype(F32))
            if strict is not None:
                w = jnp.where(strict, w, 0.0)
            wts.append(w.astype(BF16))
            vs.append(jnp.where((lane // SB_HEAD_DIM) == h, v_blk, jnp.zeros_like(v_blk)))
        acc_ref[rows] += _dot(jnp.concatenate(wts, axis=1), jnp.concatenate(vs, axis=0))

    acc_ref[...] = jnp.zeros_like(acc_ref)
    skip_ref[...] = jnp.zeros_like(skip_ref)
    tiles = [(slice(r * tk, (r + 1) * tk), d, d == r) for r in range(n_blocks) for d in reversed(range(r + 1))]
    for rows, d, _ in tiles:
        scores(qb * n_blocks + d, z_refs[d], rows)
    for rows, d, on_diag in tiles:
        block(qb * n_blocks + d, z_refs[d], rows, below_diag if on_diag else None)

    def key_block(i, d):
        return jnp.maximum((qb - i) * n_blocks - 1 - d, 0)

    def trip(c):
        i, _ = c

        def blocks(ds):
            for d in ds:
                nxt = (d + 1) % n_blocks
                scores(key_block(i + (d + 1) // n_blocks, nxt), z_refs[nxt], all_rows)
                block(key_block(i, d), z_refs[d], all_rows, None)

        blocks(range(1))
        done = jnp.min(skip_ref[...]) >= SB_EXIT_LOG2
        pl.when(jnp.logical_not(done))(lambda: blocks(range(1, n_blocks)))
        return i + 1, done.astype(jnp.int32)

    scores(key_block(0, 0), z_refs[0], all_rows)
    lax.while_loop(lambda c: jnp.logical_and(c[0] < qb, c[1] == 0), trip, (jnp.int32(0), jnp.int32(0)))
    o_ref[0] = acc_ref[...].astype(o_ref.dtype)


def _stick_breaking(q, k, v, tq):
    b, s, w = q.shape
    n_heads = LANES // SB_HEAD_DIM
    cum = (jnp.arange(SB_KEYS)[:, None] > jnp.arange(SB_KEYS)[None, :]).astype(BF16)
    kv_spec = pl.BlockSpec((1, s, LANES), lambda bi, hi, qi: (bi, 0, hi))
    q_spec = pl.BlockSpec((1, tq, LANES), lambda bi, hi, qi: (bi, qi, hi))
    return pl.pallas_call(
        functools.partial(_sb_kernel, scale=1.0 / math.sqrt(SB_HEAD_DIM)),
        grid=(b, w // LANES, s // tq),
        in_specs=[q_spec, kv_spec, kv_spec, _const_spec(cum.shape)],
        out_specs=q_spec,
        out_shape=jax.ShapeDtypeStruct((b, s, w), BF16),
        scratch_shapes=[pltpu.VMEM((tq, LANES), F32), pltpu.VMEM((n_heads, tq, 1), F32)]
        + [pltpu.VMEM((n_heads, tq, SB_KEYS), F32) for _ in range(tq // SB_KEYS)],
        compiler_params=_params("parallel", "parallel", "arbitrary"),
        name="stick_breaking",
    )(q, k, v, cum)


def _ssd_chunk(xbc, z, dt_raw, cw_ref, cb_ref, dtb_ref, a_ref, dskip_ref, gn_ref, xext_ref, state_ref):
    cl = xbc.shape[0]
    width = z.shape[1]
    gw = width // SSM_GROUPS
    n = SSM_STATE

    xext_ref[HALO:, :] = xbc
    taps = cw_ref.shape[0]
    xc = cb_ref[...] + sum(
        cw_ref[k:k + 1, :] * xext_ref[pl.ds(HALO - taps + 1 + k, cl), :] for k in range(taps))
    xext_ref[0:HALO, :] = xext_ref[cl:cl + HALO, :]
    xc = _silu(xc)
    xs = xc[:, :width]

    dt = _softplus(dt_raw + dtb_ref[...])
    da = dt * a_ref[...]
    l_idx = lax.broadcasted_iota(jnp.int32, (cl, cl), 0)
    s_idx = lax.broadcasted_iota(jnp.int32, (cl, cl), 1)
    causal = l_idx >= s_idx
    tril = causal.astype(BF16)
    a_cs = sum(_dot(tril, piece) for piece in _split3(da))
    a_cs_t = a_cs.T

    lane = lax.broadcasted_iota(jnp.int32, (cl, LANES), 1)
    low = lane < SSM_HEAD_DIM

    def expand(cols):
        return jnp.concatenate(
            [jnp.where(low, jnp.broadcast_to(cols[:, 2 * i:2 * i + 1], (cl, LANES)),
                       jnp.broadcast_to(cols[:, 2 * i + 1:2 * i + 2], (cl, LANES)))
             for i in range(SSM_HEADS // 2)], axis=1)

    dt_x = expand(dt)
    acs_x = expand(a_cs)
    decay_in = jnp.exp(acs_x)
    decay_out = jnp.exp(acs_x[cl - 1:cl, :] - acs_x)
    xdt = xs * dt_x

    heads_per_group = SSM_HEADS // SSM_GROUPS
    ys = []
    for g in range(SSM_GROUPS):
        bm = xc[:, width + g * n:width + (g + 1) * n]
        cm = xc[:, width + SSM_GROUPS * n + g * n:width + SSM_GROUPS * n + (g + 1) * n].astype(BF16)
        bm_t = bm.T.astype(BF16)
        cb = _dot(cm, bm_t)
        cols = slice(g * gw, (g + 1) * gw)
        y_diag = []
        for pair in range(heads_per_group // 2):
            ms = []
            for h in range(g * heads_per_group + 2 * pair, g * heads_per_group + 2 * pair + 2):
                seg = a_cs[:, h:h + 1] - a_cs_t[h:h + 1, :]
                decay = jnp.where(causal, jnp.exp(jnp.where(causal, seg, 0.0)), 0.0)
                ms.append((cb * decay).astype(BF16))
            c0 = g * gw + pair * LANES
            x_pair = xdt[:, c0:c0 + LANES]
            x_bd = jnp.concatenate([jnp.where(low, x_pair, 0.0), jnp.where(low, 0.0, x_pair)], axis=0)
            y_diag.append(_dot(jnp.concatenate(ms, axis=1), x_bd.astype(BF16)))
        y_diag = jnp.concatenate(y_diag, axis=1)
        state = state_ref[g]
        y_off = _dot(cm, state.astype(BF16)) * decay_in[:, cols]
        state_ref[g] = decay_in[cl - 1:cl, cols] * state + _dot(
            bm_t, (xdt[:, cols] * decay_out[:, cols]).astype(BF16))
        ys.append(y_diag + y_off)
    y = jnp.concatenate(ys, axis=1) + dskip_ref[...] * xs
    y = y * _silu(z)
    y = jnp.concatenate(
        [y[:, g * gw:(g + 1) * gw]
         * lax.rsqrt(jnp.mean(jnp.square(y[:, g * gw:(g + 1) * gw]), axis=-1, keepdims=True) + EPS)
         for g in range(SSM_GROUPS)], axis=1)
    return (y * gn_ref[...]).astype(BF16)


def _conformer_tile(glu, cw_ref, cb_ref, g_ref, b_ref, uext_ref, shift_ref):
    ts, w = glu.shape[0], glu.shape[1] // 2
    uext_ref[CONF_HALO:, :] = glu[:, :w] * _sigmoid(glu[:, w:])
    taps = cw_ref.shape[0]
    y = cb_ref[...]
    for phase in range(min(SUBLANES, taps)):
        ks = range(phase, taps, SUBLANES)
        span = ts + SUBLANES * (len(ks) - 1)
        shift_ref[0:span, :] = uext_ref[pl.ds(CONF_HALO - taps + 1 + phase, span), :]
        for a, k in enumerate(ks):
            y = y + cw_ref[k:k + 1, :] * shift_ref[SUBLANES * a:SUBLANES * a + ts, :]
    uext_ref[0:CONF_HALO, :] = uext_ref[ts:ts + CONF_HALO, :]
    yc = y - jnp.mean(y, axis=-1, keepdims=True)
    yn = yc * lax.rsqrt(jnp.mean(yc * yc, axis=-1, keepdims=True) + EPS)
    return _silu(yn * g_ref[...] + b_ref[...]).astype(BF16)


def _post_tile(h, p, m, g_post_ref, g_pre_ref, w1_ref, w2_ref, g_mlp_ref, wg_ref, wp_ref, g_ple_ref,
               hidden_chunk, side_work=()):
    side_work = list(side_work)
    h = h + _rms(m, g_post_ref[...])
    hn = _rms(h, g_pre_ref[...]).astype(BF16)
    f = jnp.zeros_like(h)
    for c in range(0, w1_ref.shape[1], hidden_chunk):
        if side_work:
            side_work.pop(0)()
        a = jnp.maximum(_dot(hn, w1_ref[:, c:c + hidden_chunk]), 0.0)
        f = f + _dot((a * a).astype(BF16), w2_ref[c:c + hidden_chunk, :])
    h = h + _rms(f, g_mlp_ref[...])
    for job in side_work:
        job()
    gate = _sigmoid(_dot(h.astype(BF16), wg_ref[...]))
    emb = _dot(p.astype(BF16), wp_ref[...])
    return h + _rms(gate * emb, g_ple_ref[...])


def _post_kernel(*refs, n_mix, hidden_chunk):
    h_ref, p_ref = refs[0], refs[1]
    y_refs = refs[2:2 + n_mix]
    wo_refs = refs[2 + n_mix:2 + 2 * n_mix]
    o_ref = refs[-1]
    m = sum(_dot(y_ref[...], wo_ref[...]) for y_ref, wo_ref in zip(y_refs, wo_refs))
    o_ref[...] = _post_tile(h_ref[...], p_ref[...], m, *refs[2 + 2 * n_mix:-1], hidden_chunk)


def _post_weights(w_out, widths, g_post, g_pre, w1, w2, g_mlp, wg, wp, g_ple):
    d = w1.shape[0]
    wos, r0 = [], 0
    for w in widths:
        wos.append(w_out[r0:r0 + w].astype(BF16))
        r0 += w
    gain = lambda g: g.reshape(1, d).astype(F32)
    return wos + [gain(g_post), gain(g_pre), w1.astype(BF16), w2.astype(BF16), gain(g_mlp),
                  wg.astype(BF16), wp.astype(BF16), gain(g_ple)]


def _post(h, p, ys, w_out, g_post, g_pre, w1, w2, g_mlp, wg, wp, g_ple, tm):
    t, d = h.shape
    weights = _post_weights(w_out, [y.shape[1] for y in ys], g_post, g_pre, w1, w2, g_mlp, wg, wp, g_ple)
    rows = lambda w: pl.BlockSpec((tm, w), lambda i: (i, 0))
    return pl.pallas_call(
        functools.partial(_post_kernel, n_mix=len(ys), hidden_chunk=min(1024, w1.shape[1])),
        grid=(t // tm,),
        in_specs=[rows(d), rows(p.shape[1])] + [rows(y.shape[1]) for y in ys]
        + [_const_spec(w.shape) for w in weights],
        out_specs=rows(d),
        out_shape=jax.ShapeDtypeStruct((t, d), F32),
        compiler_params=_params("parallel"),
        name="post_mixer",
    )(h, p, *ys, *weights)


def _ssd_kernel(xbc_ref, z_ref, dt_ref, cw_ref, cb_ref, dtb_ref, a_ref, dskip_ref, gn_ref,
                o_ref, xext_ref, state_ref):
    @pl.when(pl.program_id(1) == 0)
    def _():
        xext_ref[0:HALO, :] = jnp.zeros((HALO, xext_ref.shape[1]), F32)
        state_ref[...] = jnp.zeros_like(state_ref)

    o_ref[0] = _ssd_chunk(xbc_ref[0], z_ref[0], dt_ref[0], cw_ref, cb_ref, dtb_ref, a_ref, dskip_ref, gn_ref,
                          xext_ref, state_ref)


def _ssd(xbc, z, dt_raw, conv_w, conv_b, dt_bias, a_log, d_skip, ssm_norm):
    b, s, xw = xbc.shape
    width = z.shape[2]
    cl = SSM_CHUNK
    pad = lambda v: jnp.pad(v.astype(F32), (0, LANES - v.shape[0])).reshape(1, LANES)
    seq = lambda w: pl.BlockSpec((1, cl, w), lambda bi, ci: (bi, ci, 0))
    return pl.pallas_call(
        _ssd_kernel,
        grid=(b, s // cl),
        in_specs=[seq(xw), seq(width), seq(LANES), _const_spec(conv_w.shape), _const_spec((1, xw)),
                  _const_spec((1, LANES)), _const_spec((1, LANES)), _const_spec((1, width)),
                  _const_spec((1, width))],
        out_specs=seq(width),
        out_shape=jax.ShapeDtypeStruct((b, s, width), BF16),
        scratch_shapes=[pltpu.VMEM((cl + HALO, xw), F32),
                        pltpu.VMEM((SSM_GROUPS, SSM_STATE, width // SSM_GROUPS), F32)],
        compiler_params=_params("parallel", "arbitrary"),
        name="ssd",
    )(xbc, z, dt_raw, conv_w, conv_b.reshape(1, xw), pad(dt_bias), pad(-jnp.exp(a_log.astype(F32))),
      jnp.repeat(d_skip.astype(F32), SSM_HEAD_DIM).reshape(1, width), ssm_norm.reshape(1, width))


def _conformer_kernel(glu_ref, cw_ref, cb_ref, g_ref, b_ref, o_ref, uext_ref, shift_ref):
    @pl.when(pl.program_id(1) == 0)
    def _():
        uext_ref[0:CONF_HALO, :] = jnp.zeros((CONF_HALO, uext_ref.shape[1]), F32)

    o_ref[0] = _conformer_tile(glu_ref[0], cw_ref, cb_ref, g_ref, b_ref, uext_ref, shift_ref)


def _conformer(glu, conv_w, conv_b, ln_g, ln_b, ts):
    b, s, w2 = glu.shape
    w = w2 // 2
    return pl.pallas_call(
        _conformer_kernel,
        grid=(b, s // ts),
        in_specs=[pl.BlockSpec((1, ts, w2), lambda bi, si: (bi, si, 0)), _const_spec(conv_w.shape),
                  _const_spec((1, w)), _const_spec((1, w)), _const_spec((1, w))],
        out_specs=pl.BlockSpec((1, ts, w), lambda bi, si: (bi, si, 0)),
        out_shape=jax.ShapeDtypeStruct((b, s, w), BF16),
        scratch_shapes=[pltpu.VMEM((ts + CONF_HALO, w), F32), pltpu.VMEM((ts + CONF_HALO, w), F32)],
        compiler_params=_params("parallel", "arbitrary"),
        name="conformer",
    )(glu, conv_w, conv_b.reshape(1, w), ln_g.reshape(1, w), ln_b.reshape(1, w))


def _even_layer(h, b, s, w_in, conv_w, conv_b, ga_w, ga_b, gx_w, gx_b, lam, g_pre, tm, ts, tq):
    lw = lam.shape[0]
    sw = (w_in.shape[1] - 2 * lw) // 3
    edges = [0, lw, 2 * lw, 2 * lw + sw, 2 * lw + 2 * sw, 2 * lw + 3 * sw]
    ws = [w_in[:, a:c].astype(BF16) for a, c in zip(edges[:-1], edges[1:])]
    lru_x, gate, q, k, v = _norm_proj(h, g_pre, ws, [F32, F32, BF16, BF16, BF16], tm)
    seq = lambda a: a.reshape(b, s, a.shape[1])
    y_a = _lru(seq(lru_x), seq(gate), conv_w, conv_b, ga_w, ga_b, gx_w, gx_b, lam, ts)
    y_b = _stick_breaking(seq(q), seq(k), seq(v), tq)
    return [y_a.reshape(b * s, lw), y_b.reshape(b * s, sw)]


def _odd_layer(h, b, s, w_in, conv_w, conv_b, dt_bias, a_log, d_skip, ssm_norm,
               cm_conv_w, cm_conv_b, cm_ln_g, cm_ln_b, g_pre, tm, ts):
    width = ssm_norm.shape[0]
    xw = conv_w.shape[1]
    nh = dt_bias.shape[0]
    cw = cm_ln_g.shape[0]
    e1, e2, e3 = width, width + xw, width + xw + nh
    w_dt = jnp.pad(w_in[:, e2:e3], ((0, 0), (0, LANES - nh)))
    ws = [w.astype(BF16) for w in (w_in[:, :e1], w_in[:, e1:e2], w_dt, w_in[:, e3:])]
    z, xbc, dt_raw, glu = _norm_proj(h, g_pre, ws, [F32, F32, F32, F32], tm)
    seq = lambda a: a.reshape(b, s, a.shape[1])
    y_c = _ssd(seq(xbc), seq(z), seq(dt_raw), conv_w, conv_b, dt_bias, a_log, d_skip, ssm_norm)
    y_d = _conformer(seq(glu), cm_conv_w, cm_conv_b, cm_ln_g, cm_ln_b, ts)
    return [y_c.reshape(b * s, width), y_d.reshape(b * s, cw)]


def kernel(x, p, ev_w_in, ev_lru_conv_w, ev_lru_conv_b, ev_lru_gate_a_w, ev_lru_gate_a_b, ev_lru_gate_x_w, ev_lru_gate_x_b, ev_lru_lambda, ev_w_out, od_w_in, od_ssm_conv_w, od_ssm_conv_b, od_ssm_dt_bias, od_ssm_a_log, od_ssm_d, od_ssm_norm, od_cm_conv_w, od_cm_conv_b, od_cm_ln_g, od_cm_ln_b, od_w_out, norm_mix_pre, norm_mix_post, norm_mlp_pre, norm_mlp_post, norm_ple, mlp_w1, mlp_w2, ple_w_proj, ple_w_gate):
    b, s, d = x.shape
    depth = p.shape[0]
    t = b * s
    tm = min(512, t)
    ts = min(256, s)
    tq = min(512, s)
    h = x.reshape(t, d)
    for i in range(depth):
        j = i // 2
        p_i = p[i].reshape(t, p.shape[3])
        post_args = (norm_mix_post[i], norm_mlp_pre[i], mlp_w1[i], mlp_w2[i], norm_mlp_post[i],
                     ple_w_gate[i], ple_w_proj[i], norm_ple[i])
        if i % 2 == 0:
            ys = _even_layer(h, b, s, ev_w_in[j], ev_lru_conv_w[j], ev_lru_conv_b[j], ev_lru_gate_a_w[j],
                             ev_lru_gate_a_b[j], ev_lru_gate_x_w[j], ev_lru_gate_x_b[j], ev_lru_lambda[j],
                             norm_mix_pre[i], tm, ts, tq)
            w_out = ev_w_out[j]
        else:
            ys = _odd_layer(h, b, s, od_w_in[j], od_ssm_conv_w[j], od_ssm_conv_b[j], od_ssm_dt_bias[j],
                            od_ssm_a_log[j], od_ssm_d[j], od_ssm_norm[j], od_cm_conv_w[j], od_cm_conv_b[j],
                            od_cm_ln_g[j], od_cm_ln_b[j], norm_mix_pre[i], tm, ts)
            w_out = od_w_out[j]
        h = _post(h, p_i, ys, w_out, *post_args, tm)
    return h.reshape(b, s, d)
```
